```python
import jax, jax.numpy as jnp
from jax import lax
import numpy as np

D_MODEL = 4096
BATCH = 1
SEQ = 8192
DEPTH = 4

MEM_LEN = 256
EPS = 1e-6
CONV_WIDTH = D_MODEL // 2
CONV_KERNEL = 31
GLA_HEADS = 4
GLA_KEY = D_MODEL // 4
GLA_VALUE = D_MODEL // 2
GLA_KEY_HEAD = GLA_KEY // GLA_HEADS
GLA_VALUE_HEAD = GLA_VALUE // GLA_HEADS
GLA_LOWRANK = 16
GLA_TAU = 16.0
GLA_CHUNK = 64
HGRN_EXPAND = 128
HGRN_WIDTH = D_MODEL // 2
HGRN_HEADS = HGRN_WIDTH // HGRN_EXPAND
HGRN_CHUNK = 32
N_BRANCH = 3
BRANCH_WIDTH = D_MODEL // 2
XATTN_HEADS = 4
XATTN_DIM = D_MODEL // 4
XATTN_HEAD_DIM = XATTN_DIM // XATTN_HEADS
IN_SPLITS = (CONV_WIDTH, CONV_WIDTH, CONV_WIDTH,
             GLA_KEY, GLA_KEY, GLA_VALUE, GLA_VALUE, GLA_LOWRANK,
             HGRN_WIDTH, HGRN_WIDTH, HGRN_WIDTH, HGRN_WIDTH,
             N_BRANCH * D_MODEL)
N_IN = 3 * CONV_WIDTH + 2 * GLA_KEY + 2 * GLA_VALUE + GLA_LOWRANK + 4 * HGRN_WIDTH + N_BRANCH * D_MODEL

kernel_name = "hybrid_conv_gla_hgrn2_gated_merge"


def rmsnorm(x, g):
    xf = x.astype(jnp.float32)
    y = xf * lax.rsqrt(jnp.mean(xf * xf, axis=-1, keepdims=True) + EPS)
    return (y * g.astype(jnp.float32)).astype(x.dtype)


def layernorm(x, g, b):
    xf = x.astype(jnp.float32)
    mu = jnp.mean(xf, axis=-1, keepdims=True)
    var = jnp.mean(jnp.square(xf - mu), axis=-1, keepdims=True)
    y = (xf - mu) * lax.rsqrt(var + EPS)
    return (y * g.astype(jnp.float32) + b.astype(jnp.float32)).astype(x.dtype)


def split_columns(z):
    offsets = np.cumsum(np.array(IN_SPLITS))[:-1].tolist()
    return jnp.split(z, offsets, axis=-1)


def chunked_gated_linear_attention(q, k, v, log_decay, chunk):
    bsz, seq, heads, dk = q.shape
    dv = v.shape[-1]
    n = seq // chunk

    def to_chunks(t):
        return t.astype(jnp.float32).reshape(bsz, n, chunk, heads, t.shape[-1]).transpose(1, 0, 3, 2, 4)

    q, k, v, g = to_chunks(q), to_chunks(k), to_chunks(v), to_chunks(log_decay)
    b = jnp.cumsum(g, axis=3)
    b_ref = b[:, :, :, chunk // 2 - 1:chunk // 2, :]
    b_last = b[:, :, :, -1:, :]
    causal = jnp.tril(jnp.ones((chunk, chunk), dtype=bool))
    scores = jnp.einsum('nbhid,nbhjd->nbhij', q * jnp.exp(b - b_ref), k * jnp.exp(b_ref - b))
    o_intra = jnp.einsum('nbhij,nbhjv->nbhiv', jnp.where(causal, scores, 0.0), v)
    q_in = q * jnp.exp(b)
    k_out = k * jnp.exp(b_last - b)
    decay = jnp.exp(b_last[:, :, :, 0, :])

    def step(state, xs):
        q_c, k_c, v_c, d_c = xs
        o_c = jnp.einsum('bhid,bhdv->bhiv', q_c, state)
        state = d_c[..., None] * state + jnp.einsum('bhjd,bhjv->bhdv', k_c, v_c)
        return state, o_c

    init = jnp.zeros((bsz, heads, dk, dv), jnp.float32)
    _, o_inter = lax.scan(step, init, (q_in, k_out, v, decay))
    o = o_intra + o_inter
    return o.transpose(1, 0, 3, 2, 4).reshape(bsz, seq, heads, dv)


def conv_branch(c_val, c_glu, c_gate, conv_w, conv_b, ln_g, ln_b, pw):
    a = c_val * jax.nn.sigmoid(c_glu)
    a = lax.conv_general_dilated(a, conv_w[:, None, :], window_strides=(1,),
                                 padding=[(CONV_KERNEL - 1, 0)],
                                 dimension_numbers=('NWC', 'WIO', 'NWC'),
                                 feature_group_count=CONV_WIDTH) + conv_b
    a = jax.nn.silu(layernorm(a, ln_g, ln_b))
    return (a @ pw) * jax.nn.silu(c_gate)


def gla_branch(a_q, a_k, a_v, a_gate, a_lr, wa2, ba, norm_g):
    bsz, seq, _ = a_q.shape
    q = a_q.reshape(bsz, seq, GLA_HEADS, GLA_KEY_HEAD) * (GLA_KEY_HEAD ** -0.5)
    k = a_k.reshape(bsz, seq, GLA_HEADS, GLA_KEY_HEAD)
    v = a_v.reshape(bsz, seq, GLA_HEADS, GLA_VALUE_HEAD)
    log_alpha = jax.nn.log_sigmoid((a_lr @ wa2 + ba).astype(jnp.float32)) / GLA_TAU
    log_alpha = log_alpha.reshape(bsz, seq, GLA_HEADS, GLA_KEY_HEAD)
    o = chunked_gated_linear_attention(q, k, v, log_alpha, GLA_CHUNK)
    o = rmsnorm(o, norm_g).astype(a_v.dtype).reshape(bsz, seq, GLA_VALUE)
    return o * jax.nn.silu(a_gate)


def hgrn_branch(r_q, r_f, r_i, r_gate, lower_bound, norm_g):
    bsz, seq, _ = r_q.shape
    q = (jax.nn.silu(r_q.astype(jnp.float32)) * (HGRN_EXPAND ** -0.5)).reshape(bsz, seq, HGRN_HEADS, HGRN_EXPAND)
    f = lower_bound + (1.0 - lower_bound) * jax.nn.sigmoid(r_f.astype(jnp.float32))
    k = (1.0 - f).reshape(bsz, seq, HGRN_HEADS, HGRN_EXPAND)
    log_f = jnp.log(f).reshape(bsz, seq, HGRN_HEADS, HGRN_EXPAND)
    i = r_i.reshape(bsz, seq, HGRN_HEADS, HGRN_EXPAND)
    o = chunked_gated_linear_attention(q, k, i, log_f, HGRN_CHUNK)
    o = rmsnorm(o, norm_g).astype(r_i.dtype).reshape(bsz, seq, HGRN_WIDTH)
    return o * jax.nn.silu(r_gate)


def cross_attention(hn, memn, wq, wkv, wo):
    bsz, seq, _ = hn.shape
    mlen = memn.shape[1]
    q = (hn @ wq).reshape(bsz, seq, XATTN_HEADS, XATTN_HEAD_DIM)
    k, v = jnp.split(memn @ wkv, 2, axis=-1)
    k = k.reshape(bsz, mlen, XATTN_HEADS, XATTN_HEAD_DIM)
    v = v.reshape(bsz, mlen, XATTN_HEADS, XATTN_HEAD_DIM)
    s = jnp.einsum('bshd,bmhd->bhsm', q.astype(jnp.float32), k.astype(jnp.float32)) * (XATTN_HEAD_DIM ** -0.5)
    p = jax.nn.softmax(s, axis=-1)
    o = jnp.einsum('bhsm,bmhd->bshd', p, v.astype(jnp.float32)).astype(hn.dtype)
    return o.reshape(bsz, seq, XATTN_DIM) @ wo


def setup_inputs(seed: int = 0) -> dict:
    key = jax.random.key(seed)
    ks = jax.random.split(key, 24)

    def nrm(k, shape, scale):
        return jax.random.normal(k, shape, jnp.float32) * scale

    def gain(k, shape):
        return 1.0 + 0.02 * jax.random.normal(k, shape, jnp.float32)

    return {
        "x": nrm(ks[0], (BATCH, SEQ, D_MODEL), 1.0),
        "mem": nrm(ks[1], (BATCH, MEM_LEN, D_MODEL), 1.0),
        "norm1_g": gain(ks[2], (DEPTH, D_MODEL)),
        "w_in": nrm(ks[3], (DEPTH, D_MODEL, N_IN), D_MODEL ** -0.5),
        "conv_w": nrm(ks[4], (DEPTH, CONV_KERNEL, CONV_WIDTH), CONV_KERNEL ** -0.5),
        "conv_b": nrm(ks[5], (DEPTH, CONV_WIDTH), 0.01),
        "conv_ln_g": gain(ks[6], (DEPTH, CONV_WIDTH)),
        "conv_ln_b": nrm(ks[7], (DEPTH, CONV_WIDTH), 0.01),
        "conv_pw": nrm(ks[8], (DEPTH, CONV_WIDTH, CONV_WIDTH), CONV_WIDTH ** -0.5),
        "gla_wa2": nrm(ks[9], (DEPTH, GLA_LOWRANK, GLA_KEY), GLA_LOWRANK ** -0.5),
        "gla_ba": nrm(ks[10], (DEPTH, GLA_KEY), 0.01),
        "gla_norm_g": gain(ks[11], (DEPTH, GLA_VALUE_HEAD)),
        "hgrn_lb": nrm(ks[12], (DEPTH, HGRN_WIDTH), 0.1),
        "hgrn_norm_g": gain(ks[13], (DEPTH, HGRN_EXPAND)),
        "w_branch": nrm(ks[14], (DEPTH, N_BRANCH, BRANCH_WIDTH, D_MODEL), BRANCH_WIDTH ** -0.5),
        "w_out": nrm(ks[15], (DEPTH, D_MODEL, D_MODEL), D_MODEL ** -0.5),
        "norm2_g": gain(ks[16], (DEPTH, D_MODEL)),
        "mem_norm_g": gain(ks[17], (D_MODEL,)),
        "xq": nrm(ks[18], (DEPTH, D_MODEL, XATTN_DIM), D_MODEL ** -0.5),
        "xkv": nrm(ks[19], (DEPTH, D_MODEL, 2 * XATTN_DIM), D_MODEL ** -0.5),
        "xo": nrm(ks[20], (DEPTH, XATTN_DIM, D_MODEL), XATTN_DIM ** -0.5),
        "final_norm_g": gain(ks[21], (D_MODEL,)),
    }


def reference(x, mem, norm1_g, w_in, conv_w, conv_b, conv_ln_g, conv_ln_b, conv_pw,
              gla_wa2, gla_ba, gla_norm_g, hgrn_lb, hgrn_norm_g, w_branch, w_out,
              norm2_g, mem_norm_g, xq, xkv, xo, final_norm_g):
    bsz, seq, _ = x.shape
    lb_all = jnp.cumsum(jax.nn.softmax(hgrn_lb.astype(jnp.float32), axis=0), axis=0)
    lb_all = lb_all - lb_all[:1]
    memn = rmsnorm(mem, mem_norm_g)
    h = x
    for l in range(DEPTH):
        xn = rmsnorm(h, norm1_g[l])
        z = xn @ w_in[l]
        (c_val, c_glu, c_gate, a_q, a_k, a_v, a_gate, a_lr,
         r_q, r_f, r_i, r_gate, merge) = split_columns(z)
        y_conv = conv_branch(c_val, c_glu, c_gate, conv_w[l], conv_b[l], conv_ln_g[l], conv_ln_b[l], conv_pw[l])
        y_gla = gla_branch(a_q, a_k, a_v, a_gate, a_lr, gla_wa2[l], gla_ba[l], gla_norm_g[l])
        y_hgrn = hgrn_branch(r_q, r_f, r_i, r_gate, lb_all[l], hgrn_norm_g[l])
        branches = jnp.stack([y_conv, y_gla, y_hgrn], axis=2)
        proj = jnp.einsum('bsnc,ncd->bsnd', branches, w_branch[l])
        gates = jax.nn.sigmoid(merge.reshape(bsz, seq, N_BRANCH, D_MODEL))
        h = h + jnp.einsum('bsnd,bsnd->bsd', gates, proj) @ w_out[l]
        h = h + cross_attention(rmsnorm(h, norm2_g[l]), memn, xq[l], xkv[l], xo[l])
    return rmsnorm(h, final_norm_g)
```

```python
import functools

import jax
import jax.numpy as jnp
from jax import lax
from jax.experimental import pallas as pl
from jax.experimental.pallas import tpu as pltpu

F32 = jnp.float32
BF16 = jnp.bfloat16

EPS = 1e-6
CONV_KERNEL = 31
CONV_HALO = 32
GLA_HEADS = 4
GLA_TAU = 16.0
GLA_CHUNK = 64
HGRN_EXPAND = 128
HGRN_CHUNK = 32
N_BRANCH = 3
XATTN_HEADS = 4
LANES = 128
SUBLANES = 8
VMEM_LIMIT_BYTES = 56 * 1024 * 1024


def _params(*sem):
    return pltpu.CompilerParams(dimension_semantics=sem, vmem_limit_bytes=VMEM_LIMIT_BYTES)


def _sigmoid(x):
    return 0.5 * jnp.tanh(0.5 * x) + 0.5


def _silu(x):
    return x * _sigmoid(x)


def _dot(a, b):
    return jnp.dot(a, b, preferred_element_type=F32)


def _dot_nt(a, b):
    return lax.dot_general(a, b, (((1,), (1,)), ((), ())), preferred_element_type=F32)


def _dot_tn(a, b):
    return lax.dot_general(a, b, (((0,), (0,)), ((), ())), preferred_element_type=F32)


def _rms_kernel(x_ref, g_ref, o_ref):
    x = x_ref[...]
    ms = jnp.mean(x * x, axis=-1, keepdims=True)
    o_ref[...] = ((x * lax.rsqrt(ms + EPS)) * g_ref[...]).astype(o_ref.dtype)


def _rmsnorm(x, g, layer, out_dtype, tm=256):
    m, d = x.shape
    return pl.pallas_call(
        _rms_kernel,
        grid=(m // tm,),
        in_specs=[pl.BlockSpec((tm, d), lambda i: (i, 0)),
                  pl.BlockSpec((None, 1, d), lambda i: (layer, 0, 0))],
        out_specs=pl.BlockSpec((tm, d), lambda i: (i, 0)),
        out_shape=jax.ShapeDtypeStruct((m, d), out_dtype),
        compiler_params=_params("parallel"),
        name="rmsnorm",
    )(x, g)


def _mm_kernel(x_ref, w_ref, o_ref):
    o_ref[...] = _dot(x_ref[...], w_ref[...]).astype(o_ref.dtype)


def _mm_res_kernel(x_ref, w_ref, r_ref, o_ref):
    o_ref[...] = r_ref[...] + _dot(x_ref[...], w_ref[...])


def _matmul(x, w, layer, *, tm, tn, out_dtype=F32, residual=None, name="matmul"):
    m, k = x.shape
    n = w.shape[-1]
    in_specs = [pl.BlockSpec((tm, k), lambda j, i: (i, 0)),
                pl.BlockSpec((None, k, tn), lambda j, i: (layer, 0, j))]
    args = [x, w]
    body = _mm_kernel
    if residual is not None:
        in_specs.append(pl.BlockSpec((tm, tn), lambda j, i: (i, j)))
        args.append(residual)
        body = _mm_res_kernel
    return pl.pallas_call(
        body,
        grid=(n // tn, m // tm),
        in_specs=in_specs,
        out_specs=pl.BlockSpec((tm, tn), lambda j, i: (i, j)),
        out_shape=jax.ShapeDtypeStruct((m, n), out_dtype),
        compiler_params=_params("parallel", "parallel"),
        name=name,
    )(*args)


def _conv_kernel(val_ref, glu_ref, gate_ref, pval_ref, pglu_ref, cw_ref, cb_ref, lg_ref, lb_ref,
                 pw_ref, o_ref, buf_ref, acc_ref, *, tm, width):
    rows = 64
    i = pl.program_id(0)
    a_prev = pval_ref[...] * _sigmoid(pglu_ref[...])
    buf_ref[0:CONV_HALO, :] = jnp.where(i > 0, a_prev, 0.0)
    buf_ref[CONV_HALO:CONV_HALO + tm, :] = val_ref[...] * _sigmoid(glu_ref[...])
    first = CONV_HALO - (CONV_KERNEL - 1)

    def lane_block(cb, carry):
        lanes = pl.ds(pl.multiple_of(cb * LANES, LANES), LANES)
        for rb in range(tm // rows):
            acc = jnp.zeros((rows, LANES), F32)
            for r in range(SUBLANES):
                js = [j for j in range((first + CONV_KERNEL) // SUBLANES + 1)
                      if first <= r + SUBLANES * j < first + CONV_KERNEL]
                win = buf_ref[pl.ds(rb * rows + r, rows + SUBLANES * max(js)), lanes]
                for j in js:
                    k = r + SUBLANES * j - first
                    acc = acc + cw_ref[k:k + 1, lanes] * win[SUBLANES * j:SUBLANES * j + rows, :]
            acc_ref[rb * rows:(rb + 1) * rows, lanes] = acc + cb_ref[:, lanes]
        return carry

    lax.fori_loop(0, width // LANES, lane_block, 0)

    c = acc_ref[...]
    mu = jnp.mean(c, axis=-1, keepdims=True)
    xc = c - mu
    var = jnp.mean(xc * xc, axis=-1, keepdims=True)
    y = (xc * lax.rsqrt(var + EPS)) * lg_ref[...] + lb_ref[...]
    pw_out = _dot(_silu(y).astype(BF16), pw_ref[...])
    o_ref[...] = (pw_out * _silu(gate_ref[...])).astype(o_ref.dtype)


def _conv_branch(z, layer, conv_w, conv_b, ln_g, ln_b, pw, *, tm=256):
    s = z.shape[0]
    width = pw.shape[-1]
    halo_blocks = tm // CONV_HALO
    vec = pl.BlockSpec((None, 1, width), lambda i: (layer, 0, 0))
    return pl.pallas_call(
        functools.partial(_conv_kernel, tm=tm, width=width),
        grid=(s // tm,),
        in_specs=[pl.BlockSpec((tm, width), lambda i: (i, 0)),
                  pl.BlockSpec((tm, width), lambda i: (i, 1)),
                  pl.BlockSpec((tm, width), lambda i: (i, 2)),
                  pl.BlockSpec((CONV_HALO, width), lambda i: (jnp.maximum(i * halo_blocks - 1, 0), 0)),
                  pl.BlockSpec((CONV_HALO, width), lambda i: (jnp.maximum(i * halo_blocks - 1, 0), 1)),
                  pl.BlockSpec((None, CONV_KERNEL, width), lambda i: (layer, 0, 0)),
                  vec, vec, vec,
                  pl.BlockSpec((None, width, width), lambda i: (layer, 0, 0))],
        out_specs=pl.BlockSpec((tm, width), lambda i: (i, 0)),
        out_shape=jax.ShapeDtypeStruct((s, width), BF16),
        scratch_shapes=[pltpu.VMEM((CONV_HALO + tm, width), F32),
                        pltpu.VMEM((tm, width), F32)],
        compiler_params=_params("parallel"),
        name="conv_branch",
    )(z, z, z, z, z, conv_w, conv_b, ln_g, ln_b, pw)


def _chunk_matrices(t, chunk):
    row = lax.broadcasted_iota(jnp.int32, (t, t), 0)
    col = lax.broadcasted_iota(jnp.int32, (t, t), 1)
    same = (row // chunk) == (col // chunk)
    causal = same & (col <= row)
    upto_mid = same & ((col % chunk) <= chunk // 2 - 1)
    one = jnp.ones((t, t), F32)
    zero = jnp.zeros((t, t), F32)
    cum = jnp.where(causal, one, zero)
    to_mid = cum - jnp.where(upto_mid, one, zero)
    to_end = jnp.where(same, one, zero) - cum
    return causal, cum.astype(BF16), to_mid.astype(BF16), to_end.astype(BF16)


def _split3(g):
    g1 = g.astype(BF16)
    r1 = g - g1.astype(F32)
    g2 = r1.astype(BF16)
    g3 = (r1 - g2.astype(F32)).astype(BF16)
    return g1, g2, g3


def _chunk_attention(q, k, v, g, st_ref, st_idx, causal, cum, to_mid, to_end, chunk):
    t = q.shape[0]
    parts = _split3(g)

    def prefix(mat):
        return _dot(mat, parts[0]) + _dot(mat, parts[1]) + _dot(mat, parts[2])

    b = prefix(cum)
    d_mid = prefix(to_mid)
    d_end = prefix(to_end)
    vb = v.astype(BF16)
    scores = _dot_nt((q * jnp.exp(d_mid)).astype(BF16), (k * jnp.exp(-d_mid)).astype(BF16))
    o_intra = _dot(jnp.where(causal, scores, 0.0).astype(BF16), vb)
    q_in = (q * jnp.exp(b)).astype(BF16)
    k_out = (k * jnp.exp(d_end)).astype(BF16)
    outs = []
    state = st_ref[st_idx]
    for c in range(t // chunk):
        rows = slice(c * chunk, (c + 1) * chunk)
        outs.append(o_intra[rows] + _dot_nt(q_in[rows], state.astype(BF16)))
        decay = jnp.exp(b[(c + 1) * chunk - 1:(c + 1) * chunk, :])
        state = state * decay + _dot_tn(vb[rows], k_out[rows])
    st_ref[st_idx] = state
    return jnp.concatenate(outs, axis=0)


def _gla_kernel(q_ref, k_ref, v_ref, gate_ref, lr_ref, wa2_ref, ba_ref, ng_ref, o_ref, st_ref, *, t):
    @pl.when(pl.program_id(0) == 0)
    def _():
        st_ref[...] = jnp.zeros_like(st_ref)

    dk = q_ref.shape[1] // GLA_HEADS
    dv = v_ref.shape[1] // GLA_HEADS
    causal, cum, to_mid, to_end = _chunk_matrices(t, GLA_CHUNK)
    x = _dot(lr_ref[...].astype(BF16), wa2_ref[...]) + ba_ref[...]
    g_all = (jnp.minimum(x, 0.0) - jnp.log1p(jnp.exp(-jnp.abs(x)))) * (1.0 / GLA_TAU)
    for h in range(GLA_HEADS):
        ks = slice(h * dk, (h + 1) * dk)
        vs = slice(h * dv, (h + 1) * dv)
        q = q_ref[:, ks] * (dk ** -0.5)
        o = _chunk_attention(q, k_ref[:, ks], v_ref[:, vs], g_all[:, ks], st_ref, h,
                             causal, cum, to_mid, to_end, GLA_CHUNK)
        ms = jnp.mean(o * o, axis=-1, keepdims=True)
        o = (o * lax.rsqrt(ms + EPS)) * ng_ref[...]
        o_ref[:, vs] = (o * _silu(gate_ref[:, vs])).astype(o_ref.dtype)


def _gla_branch(z, z_lr, layer, wa2, ba, norm_g, *, col0, key_width, value_width, t=256):
    s = z.shape[0]
    dk = key_width // GLA_HEADS
    dv = value_width // GLA_HEADS
    qb = col0 // key_width
    vb = (col0 + 2 * key_width) // value_width
    return pl.pallas_call(
        functools.partial(_gla_kernel, t=t),
        grid=(s // t,),
        in_specs=[pl.BlockSpec((t, key_width), lambda i: (i, qb)),
                  pl.BlockSpec((t, key_width), lambda i: (i, qb + 1)),
                  pl.BlockSpec((t, value_width), lambda i: (i, vb)),
                  pl.BlockSpec((t, value_width), lambda i: (i, vb + 1)),
                  pl.BlockSpec((t, LANES), lambda i: (i, 0)),
                  pl.BlockSpec((None, LANES, key_width), lambda i: (layer, 0, 0)),
                  pl.BlockSpec((None, 1, key_width), lambda i: (layer, 0, 0)),
                  pl.BlockSpec((None, 1, dv), lambda i: (layer, 0, 0))],
        out_specs=pl.BlockSpec((t, value_width), lambda i: (i, 0)),
        out_shape=jax.ShapeDtypeStruct((s, value_width), BF16),
        scratch_shapes=[pltpu.VMEM((GLA_HEADS, dv, dk), F32)],
        compiler_params=_params("arbitrary"),
        name="gla_branch",
    )(z, z, z, z, z_lr, wa2, ba, norm_g)


def _hgrn_kernel(q_ref, f_ref, i_ref, gate_ref, lbp_ref, ng_ref, o_ref, st_ref, *, t, layer):
    @pl.when(pl.program_id(0) == 0)
    def _():
        st_ref[...] = jnp.zeros_like(st_ref)

    d = HGRN_EXPAND
    depth = lbp_ref.shape[0]
    causal, cum, to_mid, to_end = _chunk_matrices(t, HGRN_CHUNK)

    def head(h, carry):
        lanes = pl.ds(pl.multiple_of(h * d, d), d)
        p = lbp_ref[:, lanes]
        e = jnp.exp(p - jnp.max(p, axis=0, keepdims=True))
        sm = e / jnp.sum(e, axis=0, keepdims=True)
        layer_ids = lax.broadcasted_iota(jnp.int32, (depth, d), 0)
        lb = jnp.sum(jnp.where((layer_ids >= 1) & (layer_ids <= layer), sm, 0.0), axis=0, keepdims=True)
        q = _silu(q_ref[:, lanes]) * (d ** -0.5)
        f = lb + (1.0 - lb) * (1.0 / (1.0 + jnp.exp(-f_ref[:, lanes])))
        o = _chunk_attention(q, 1.0 - f, i_ref[:, lanes], jnp.log(f), st_ref, h,
                             causal, cum, to_mid, to_end, HGRN_CHUNK)
        ms = jnp.mean(o * o, axis=-1, keepdims=True)
        o = (o * lax.rsqrt(ms + EPS)) * ng_ref[...]
        o_ref[:, lanes] = (o * _silu(gate_ref[:, lanes])).astype(o_ref.dtype)
        return carry

    lax.fori_loop(0, q_ref.shape[1] // d, head, 0)


def _hgrn_branch(z, layer, lb_param, norm_g, *, col0, width, t=256):
    s = z.shape[0]
    b0 = col0 // width
    depth = lb_param.shape[0]
    return pl.pallas_call(
        functools.partial(_hgrn_kernel, t=t, layer=layer),
        grid=(s // t,),
        in_specs=[pl.BlockSpec((t, width), lambda i: (i, b0)),
                  pl.BlockSpec((t, width), lambda i: (i, b0 + 1)),
                  pl.BlockSpec((t, width), lambda i: (i, b0 + 2)),
                  pl.BlockSpec((t, width), lambda i: (i, b0 + 3)),
                  pl.BlockSpec((depth, width), lambda i: (0, 0)),
                  pl.BlockSpec((None, 1, HGRN_EXPAND), lambda i: (layer, 0, 0))],
        out_specs=pl.BlockSpec((t, width), lambda i: (i, 0)),
        out_shape=jax.ShapeDtypeStruct((s, width), BF16),
        scratch_shapes=[pltpu.VMEM((width // HGRN_EXPAND, HGRN_EXPAND, HGRN_EXPAND), F32)],
        compiler_params=_params("arbitrary"),
        name="hgrn_branch",
    )(z, z, z, z, lb_param, norm_g)


def _merge_kernel(xn_ref, wm0_ref, wm1_ref, wm2_ref, y0_ref, y1_ref, y2_ref, wb_ref, o_ref):
    xn = xn_ref[...]
    acc = None
    for b, (wm_ref, y_ref) in enumerate(((wm0_ref, y0_ref), (wm1_ref, y1_ref), (wm2_ref, y2_ref))):
        term = _sigmoid(_dot(xn, wm_ref[...])) * _dot(y_ref[...], wb_ref[b])
        acc = term if acc is None else acc + term
    o_ref[...] = acc.astype(o_ref.dtype)


def _merge(xn, w_merge, ys, w_branch, layer, *, tm=512, tn=256):
    s, d = xn.shape
    cw = ys[0].shape[1]
    nb = d // tn

    def wm_spec(b):
        return pl.BlockSpec((None, d, tn), lambda j, i: (layer, 0, b * nb + j))

    y_spec = pl.BlockSpec((tm, cw), lambda j, i: (i, 0))
    return pl.pallas_call(
        _merge_kernel,
        grid=(nb, s // tm),
        in_specs=[pl.BlockSpec((tm, d), lambda j, i: (i, 0)),
                  wm_spec(0), wm_spec(1), wm_spec(2),
                  y_spec, y_spec, y_spec,
                  pl.BlockSpec((None, N_BRANCH, cw, tn), lambda j, i: (layer, 0, 0, j))],
        out_specs=pl.BlockSpec((tm, tn), lambda j, i: (i, j)),
        out_shape=jax.ShapeDtypeStruct((s, d), BF16),
        compiler_params=_params("parallel", "parallel"),
        name="gated_merge",
    )(xn, w_merge, w_merge, w_merge, *ys, w_branch)


def _xattn_kernel(h_ref, g_ref, wq_ref, kv_ref, wo_ref, o_ref):
    h = h_ref[...]
    ms = jnp.mean(h * h, axis=-1, keepdims=True)
    hn = ((h * lax.rsqrt(ms + EPS)) * g_ref[...]).astype(BF16)
    q = _dot(hn, wq_ref[...]).astype(BF16)
    xd = wq_ref.shape[1]
    hd = xd // XATTN_HEADS
    outs = []
    for a in range(XATTN_HEADS):
        s = _dot_nt(q[:, a * hd:(a + 1) * hd], kv_ref[:, a * hd:(a + 1) * hd]) * (hd ** -0.5)
        e = jnp.exp(s - jnp.max(s, axis=-1, keepdims=True))
        p = e / jnp.sum(e, axis=-1, keepdims=True)
        outs.append(_dot(p.astype(BF16), kv_ref[:, xd + a * hd:xd + (a + 1) * hd]))
    o = jnp.concatenate(outs, axis=-1).astype(BF16)
    o_ref[...] = h + _dot(o, wo_ref[...])


def _cross_attention(h, norm_g, wq, kv, wo, layer, *, tm=256):
    s, d = h.shape
    xd = wq.shape[-1]
    once = pl.Buffered(1)
    return pl.pallas_call(
        _xattn_kernel,
        grid=(s // tm,),
        in_specs=[pl.BlockSpec((tm, d), lambda i: (i, 0)),
                  pl.BlockSpec((None, 1, d), lambda i: (layer, 0, 0)),
                  pl.BlockSpec((None, d, xd), lambda i: (layer, 0, 0), pipeline_mode=once),
                  pl.BlockSpec(kv.shape, lambda i: (0, 0), pipeline_mode=once),
                  pl.BlockSpec((None, xd, d), lambda i: (layer, 0, 0), pipeline_mode=once)],
        out_specs=pl.BlockSpec((tm, d), lambda i: (i, 0)),
        out_shape=jax.ShapeDtypeStruct((s, d), F32),
        compiler_params=_params("parallel"),
        name="cross_attention",
    )(h, norm_g, wq, kv, wo)


def kernel(x, mem, norm1_g, w_in, conv_w, conv_b, conv_ln_g, conv_ln_b, conv_pw, gla_wa2, gla_ba,
           gla_norm_g, hgrn_lb, hgrn_norm_g, w_branch, w_out, norm2_g, mem_norm_g, xq, xkv, xo,
           final_norm_g):
    bsz, seq, d = x.shape
    assert bsz == 1
    depth = w_in.shape[0]
    cw = conv_pw.shape[-1]
    key_w = gla_wa2.shape[-1]
    val_w = gla_norm_g.shape[-1] * GLA_HEADS
    lowrank = gla_wa2.shape[1]
    hg_w = hgrn_lb.shape[-1]
    col_lr = 3 * cw + 2 * key_w + 2 * val_w
    col_hgrn = col_lr + lowrank
    col_merge = col_hgrn + 4 * hg_w

    def vec(a):
        return a.reshape(a.shape[0], 1, a.shape[-1])

    w_main = jnp.concatenate([w_in[:, :, :col_lr], w_in[:, :, col_hgrn:col_merge]], axis=-1).astype(BF16)
    w_lr = jnp.pad(w_in[:, :, col_lr:col_hgrn], ((0, 0), (0, 0), (0, LANES - lowrank))).astype(BF16)
    w_merge = w_in[:, :, col_merge:].astype(BF16)
    wa2 = jnp.pad(gla_wa2, ((0, 0), (0, LANES - lowrank), (0, 0))).astype(BF16)
    pw = conv_pw.astype(BF16)
    wb = w_branch.astype(BF16)
    wo = w_out.astype(BF16)
    wq = xq.astype(BF16)
    wkv = xkv.astype(BF16)
    wxo = xo.astype(BF16)

    memn = _rmsnorm(mem[0], mem_norm_g.reshape(1, 1, d), 0, BF16, tm=mem.shape[1])
    h = x[0]
    col_gla = 3 * cw
    col_hg = col_gla + 2 * key_w + 2 * val_w
    for l in range(depth):
        xn = _rmsnorm(h, vec(norm1_g), l, BF16)
        z = _matmul(xn, w_main, l, tm=512, tn=1024, name="in_proj")
        z_lr = _matmul(xn, w_lr, l, tm=512, tn=LANES, name="in_proj_lowrank")
        y_conv = _conv_branch(z, l, conv_w, vec(conv_b), vec(conv_ln_g), vec(conv_ln_b), pw)
        y_gla = _gla_branch(z, z_lr, l, wa2, vec(gla_ba), vec(gla_norm_g),
                            col0=col_gla, key_width=key_w, value_width=val_w)
        y_hgrn = _hgrn_branch(z, l, hgrn_lb, vec(hgrn_norm_g), col0=col_hg, width=hg_w)
        m = _merge(xn, w_merge, (y_conv, y_gla, y_hgrn), wb, l)
        h = _matmul(m, wo, l, tm=512, tn=1024, residual=h, name="out_proj")
        kv = _matmul(memn, wkv, l, tm=mem.shape[1], tn=1024, out_dtype=BF16, name="kv_proj")
        h = _cross_attention(h, vec(norm2_g), wq, kv, wxo, l)
    out = _rmsnorm(h, final_norm_g.reshape(1, 1, d), 0, F32)
    return out.reshape(bsz, seq, d)
```

```python
import functools

import jax
import jax.numpy as jnp
from jax import lax
from jax.experimental import pallas as pl
from jax.experimental.pallas import tpu as pltpu

F32 = jnp.float32
BF16 = jnp.bfloat16

EPS = 1e-6
CONV_KERNEL = 31
CONV_HALO = 32
GLA_HEADS = 4
GLA_TAU = 16.0
GLA_CHUNK = 64
HGRN_EXPAND = 128
HGRN_CHUNK = 32
N_BRANCH = 3
XATTN_HEADS = 4
LANES = 128
SUBLANES = 8
VMEM_LIMIT_BYTES = 56 * 1024 * 1024


def _params(*sem):
    return pltpu.CompilerParams(dimension_semantics=sem, vmem_limit_bytes=VMEM_LIMIT_BYTES)


def _sigmoid(x):
    return 0.5 * jnp.tanh(0.5 * x) + 0.5


def _silu(x):
    return x * _sigmoid(x)


def _dot(a, b):
    return jnp.dot(a, b, preferred_element_type=F32)


def _dot_nt(a, b):
    return lax.dot_general(a, b, (((1,), (1,)), ((), ())), preferred_element_type=F32)


def _dot_tn(a, b):
    return lax.dot_general(a, b, (((0,), (0,)), ((), ())), preferred_element_type=F32)


def _rms(x, g):
    ms = jnp.mean(x * x, axis=-1, keepdims=True)
    return (x * lax.rsqrt(ms + EPS)) * g


def _rms_kernel(x_ref, g_ref, o_ref):
    o_ref[...] = _rms(x_ref[...], g_ref[...]).astype(o_ref.dtype)


def _rmsnorm(x, g, layer, out_dtype, tm=256):
    m, d = x.shape
    return pl.pallas_call(
        _rms_kernel,
        grid=(m // tm,),
        in_specs=[pl.BlockSpec((tm, d), lambda i: (i, 0)),
                  pl.BlockSpec((None, 1, d), lambda i: (layer, 0, 0))],
        out_specs=pl.BlockSpec((tm, d), lambda i: (i, 0)),
        out_shape=jax.ShapeDtypeStruct((m, d), out_dtype),
        compiler_params=_params("parallel"),
        name="rmsnorm",
    )(x, g)


def _realign_kernel(a_ref, b_ref, o_ref, *, shift):
    w = jnp.concatenate([a_ref[...], b_ref[...]], axis=1)
    width = w.shape[1]
    o_ref[...] = pltpu.roll(w, width - shift, axis=1)[:, :a_ref.shape[1]].astype(o_ref.dtype)


def _realign_columns(w, col0, ncols, *, tk=512, tn=1024):
    depth, k, _ = w.shape
    shift = col0 % LANES
    base = col0 - shift
    assert base % tn == 0 and ncols % tn == 0
    return pl.pallas_call(
        functools.partial(_realign_kernel, shift=shift),
        grid=(depth, k // tk, ncols // tn),
        in_specs=[pl.BlockSpec((None, tk, tn), lambda l, r, j: (l, r, base // tn + j)),
                  pl.BlockSpec((None, tk, LANES), lambda l, r, j: (l, r, (base + (j + 1) * tn) // LANES))],
        out_specs=pl.BlockSpec((None, tk, tn), lambda l, r, j: (l, r, j)),
        out_shape=jax.ShapeDtypeStruct((depth, k, ncols), BF16),
        compiler_params=_params("parallel", "parallel", "parallel"),
        name="realign_weights",
    )(w, w)


def _mm_kernel(x_ref, w_ref, *rest, residual, cast_weight):
    rest = list(rest)
    r_ref = rest.pop(0) if residual else None
    o_ref = rest.pop(0)
    if cast_weight:
        wb_ref = rest.pop(0)

        @pl.when(pl.program_id(1) == 0)
        def _():
            wb_ref[...] = w_ref[...].astype(BF16)

        w = wb_ref[...]
    else:
        w = w_ref[...]
    acc = _dot(x_ref[...], w)
    if residual:
        acc = r_ref[...] + acc
    o_ref[...] = acc.astype(o_ref.dtype)


def _matmul(x, w, layer, *, tm, tn, col0=0, ncols=None, out_dtype=F32, residual=None, name="matmul"):
    m, k = x.shape
    ncols = w.shape[-1] - col0 if ncols is None else ncols
    assert col0 % tn == 0 and ncols % tn == 0
    cast_weight = w.dtype != BF16
    in_specs = [pl.BlockSpec((tm, k), lambda j, i: (i, 0)),
                pl.BlockSpec((None, k, tn), lambda j, i: (layer, 0, col0 // tn + j))]
    args = [x, w]
    if residual is not None:
        in_specs.append(pl.BlockSpec((tm, tn), lambda j, i: (i, j)))
        args.append(residual)
    return pl.pallas_call(
        functools.partial(_mm_kernel, residual=residual is not None, cast_weight=cast_weight),
        grid=(ncols // tn, m // tm),
        in_specs=in_specs,
        out_specs=pl.BlockSpec((tm, tn), lambda j, i: (i, j)),
        out_shape=jax.ShapeDtypeStruct((m, ncols), out_dtype),
        scratch_shapes=[pltpu.VMEM((k, tn), BF16)] if cast_weight else [],
        compiler_params=_params("parallel", "arbitrary"),
        name=name,
    )(*args)


def _conv_kernel(val_ref, glu_ref, gate_ref, pval_ref, pglu_ref, cw_ref, cb_ref, lg_ref, lb_ref,
                 pw_ref, o_ref, buf_ref, acc_ref, pwb_ref, *, tm, width):
    rows = 64
    i = pl.program_id(0)

    @pl.when(i == 0)
    def _():
        pwb_ref[...] = pw_ref[...].astype(BF16)

    a_prev = pval_ref[...] * _sigmoid(pglu_ref[...])
    buf_ref[0:CONV_HALO, :] = jnp.where(i > 0, a_prev, 0.0)
    buf_ref[CONV_HALO:CONV_HALO + tm, :] = val_ref[...] * _sigmoid(glu_ref[...])
    first = CONV_HALO - (CONV_KERNEL - 1)
    strip = rows + CONV_HALO + SUBLANES

    def lane_block(cb, carry):
        lanes = pl.ds(pl.multiple_of(cb * LANES, LANES), LANES)
        for rb in range(tm // rows):
            base = min(rb * rows, CONV_HALO + tm - strip)
            off = rb * rows - base
            x = buf_ref[base:base + strip, lanes]
            acc = jnp.zeros((rows, LANES), F32)
            for r in range(SUBLANES):
                xr = x if r == 0 else pltpu.roll(x, strip - r, axis=0)
                for j in range(strip // SUBLANES):
                    k = r + SUBLANES * j - first
                    if 0 <= k < CONV_KERNEL:
                        lo = off + SUBLANES * j
                        acc = acc + cw_ref[k:k + 1, lanes] * xr[lo:lo + rows, :]
            acc_ref[rb * rows:(rb + 1) * rows, lanes] = acc + cb_ref[:, lanes]
        return carry

    lax.fori_loop(0, width // LANES, lane_block, 0)

    c = acc_ref[...]
    mu = jnp.mean(c, axis=-1, keepdims=True)
    xc = c - mu
    var = jnp.mean(xc * xc, axis=-1, keepdims=True)
    y = (xc * lax.rsqrt(var + EPS)) * lg_ref[...] + lb_ref[...]
    pw_out = _dot(_silu(y).astype(BF16), pwb_ref[...])
    o_ref[...] = (pw_out * _silu(gate_ref[...])).astype(o_ref.dtype)


def _conv_branch(z, layer, conv_w, conv_b, ln_g, ln_b, pw, *, tm=256):
    s = z.shape[0]
    width = pw.shape[-1]
    halo_blocks = tm // CONV_HALO
    vec = pl.BlockSpec((None, 1, width), lambda i: (layer, 0, 0))
    return pl.pallas_call(
        functools.partial(_conv_kernel, tm=tm, width=width),
        grid=(s // tm,),
        in_specs=[pl.BlockSpec((tm, width), lambda i: (i, 0)),
                  pl.BlockSpec((tm, width), lambda i: (i, 1)),
                  pl.BlockSpec((tm, width), lambda i: (i, 2)),
                  pl.BlockSpec((CONV_HALO, width), lambda i: (jnp.maximum(i * halo_blocks - 1, 0), 0)),
                  pl.BlockSpec((CONV_HALO, width), lambda i: (jnp.maximum(i * halo_blocks - 1, 0), 1)),
                  pl.BlockSpec((None, CONV_KERNEL, width), lambda i: (layer, 0, 0)),
                  vec, vec, vec,
                  pl.BlockSpec((None, width, width), lambda i: (layer, 0, 0), pipeline_mode=pl.Buffered(1))],
        out_specs=pl.BlockSpec((tm, width), lambda i: (i, 0)),
        out_shape=jax.ShapeDtypeStruct((s, width), BF16),
        scratch_shapes=[pltpu.VMEM((CONV_HALO + tm, width), F32),
                        pltpu.VMEM((tm, width), F32),
                        pltpu.VMEM((width, width), BF16)],
        compiler_params=_params("arbitrary"),
        name="conv_branch",
    )(z, z, z, z, z, conv_w, conv_b, ln_g, ln_b, pw)


def _chunk_masks(t, chunk):
    row = lax.broadcasted_iota(jnp.int32, (t, t), 0)
    col = lax.broadcasted_iota(jnp.int32, (t, t), 1)
    return ((row // chunk) == (col // chunk)) & (col <= row)


def _chunk_cumsum(g, chunk):
    pos = lax.broadcasted_iota(jnp.int32, g.shape, 0) % chunk
    b = g
    s = 1
    while s < chunk:
        b = b + jnp.where(pos >= s, pltpu.roll(b, s, axis=0), 0.0)
        s *= 2
    return b


def _chunk_row(b, chunk, idx):
    t, w = b.shape
    b3 = b.reshape(t // chunk, chunk, w)
    return jnp.broadcast_to(b3[:, idx:idx + 1, :], b3.shape).reshape(t, w)


def _chunk_attention(q, k, v, g, state, causal, chunk, group):
    t, dk = q.shape
    dv = v.shape[1]
    n_chunks = t // chunk
    span = group * chunk
    b = _chunk_cumsum(g, chunk)
    d_mid = b - _chunk_row(b, chunk, chunk // 2 - 1)
    d_end = _chunk_row(b, chunk, chunk - 1) - b
    vb = v.astype(BF16)
    scores = _dot_nt((q * jnp.exp(d_mid)).astype(BF16), (k * jnp.exp(-d_mid)).astype(BF16))
    o_intra = _dot(jnp.where(causal, scores, 0.0).astype(BF16), vb)
    q_in = (q * jnp.exp(b)).astype(BF16)
    k_out = (k * jnp.exp(d_end)).astype(BF16)

    slot = lax.broadcasted_iota(jnp.int32, (span, dk), 0) // chunk
    incs = []
    for n in range(n_chunks // group):
        rows = slice(n * span, (n + 1) * span)
        ko = k_out[rows]
        if group == 1:
            incs.append(_dot_tn(vb[rows], ko))
            continue
        blocks = jnp.concatenate([jnp.where(slot == u, ko, jnp.zeros_like(ko)) for u in range(group)], axis=1)
        u_all = _dot_tn(vb[rows], blocks)
        incs.extend(u_all[:, u * dk:(u + 1) * dk] for u in range(group))

    starts = []
    for c in range(n_chunks):
        starts.append(state)
        decay = jnp.exp(b[(c + 1) * chunk - 1:(c + 1) * chunk, :])
        state = state * decay + incs[c]

    outs = []
    for n in range(n_chunks // group):
        rows = slice(n * span, (n + 1) * span)
        if group == 1:
            outs.append(o_intra[rows] + _dot_nt(q_in[rows], starts[n].astype(BF16)))
            continue
        stacked = jnp.concatenate(starts[n * group:(n + 1) * group], axis=0).astype(BF16)
        r = _dot_nt(q_in[rows], stacked)
        for u in range(group):
            sub = slice(n * span + u * chunk, n * span + (u + 1) * chunk)
            outs.append(o_intra[sub] + r[u * chunk:(u + 1) * chunk, u * dv:(u + 1) * dv])
    return jnp.concatenate(outs, axis=0), state


def _gla_kernel(q_ref, k_ref, v_ref, gate_ref, lr_ref, wa2_ref, ba_ref, ng_ref, o_ref, st_ref, *, t):
    @pl.when(pl.program_id(0) == 0)
    def _():
        st_ref[...] = jnp.zeros_like(st_ref)

    dk = q_ref.shape[1] // GLA_HEADS
    dv = v_ref.shape[1] // GLA_HEADS
    causal = _chunk_masks(t, GLA_CHUNK)
    x = _dot(lr_ref[...].astype(BF16), wa2_ref[...]) + ba_ref[...]
    g_all = (jnp.minimum(x, 0.0) - jnp.log1p(jnp.exp(-jnp.abs(x)))) * (1.0 / GLA_TAU)
    for h in range(GLA_HEADS):
        ks = slice(h * dk, (h + 1) * dk)
        vs = slice(h * dv, (h + 1) * dv)
        q = q_ref[:, ks] * (dk ** -0.5)
        o, st_ref[h] = _chunk_attention(q, k_ref[:, ks], v_ref[:, vs], g_all[:, ks], st_ref[h],
                                        causal, GLA_CHUNK, 1)
        o_ref[:, vs] = (_rms(o, ng_ref[...]) * _silu(gate_ref[:, vs])).astype(o_ref.dtype)


def _gla_branch(z, z_lr, layer, wa2, ba, norm_g, *, col0, key_width, value_width, t=256):
    s = z.shape[0]
    dk = key_width // GLA_HEADS
    dv = value_width // GLA_HEADS
    qb = col0 // key_width
    vb = (col0 + 2 * key_width) // value_width
    return pl.pallas_call(
        functools.partial(_gla_kernel, t=t),
        grid=(s // t,),
        in_specs=[pl.BlockSpec((t, key_width), lambda i: (i, qb)),
                  pl.BlockSpec((t, key_width), lambda i: (i, qb + 1)),
                  pl.BlockSpec((t, value_width), lambda i: (i, vb)),
                  pl.BlockSpec((t, value_width), lambda i: (i, vb + 1)),
                  pl.BlockSpec((t, LANES), lambda i: (i, 0)),
                  pl.BlockSpec((None, LANES, key_width), lambda i: (layer, 0, 0)),
                  pl.BlockSpec((None, 1, key_width), lambda i: (layer, 0, 0)),
                  pl.BlockSpec((None, 1, dv), lambda i: (layer, 0, 0))],
        out_specs=pl.BlockSpec((t, value_width), lambda i: (i, 0)),
        out_shape=jax.ShapeDtypeStruct((s, value_width), BF16),
        scratch_shapes=[pltpu.VMEM((GLA_HEADS, dv, dk), F32)],
        compiler_params=_params("arbitrary"),
        name="gla_branch",
    )(z, z, z, z, z_lr, wa2, ba, norm_g)


def _hgrn_kernel(q_ref, f_ref, i_ref, gate_ref, lbp_ref, ng_ref, o_ref, st_ref, *, t, layer, heads_per_step, group):
    @pl.when(pl.program_id(0) == 0)
    def _():
        st_ref[...] = jnp.zeros_like(st_ref)

    d = HGRN_EXPAND
    depth = lbp_ref.shape[0]
    causal = _chunk_masks(t, HGRN_CHUNK)

    def one_head(h):
        lanes = pl.ds(pl.multiple_of(h * d, d), d)
        p = lbp_ref[:, lanes]
        e = jnp.exp(p - jnp.max(p, axis=0, keepdims=True))
        sm = e / jnp.sum(e, axis=0, keepdims=True)
        layer_ids = lax.broadcasted_iota(jnp.int32, (depth, d), 0)
        lb = jnp.sum(jnp.where((layer_ids >= 1) & (layer_ids <= layer), sm, 0.0), axis=0, keepdims=True)
        q = _silu(q_ref[:, lanes]) * (d ** -0.5)
        f = lb + (1.0 - lb) * (1.0 / (1.0 + jnp.exp(-f_ref[:, lanes])))
        o, st_ref[h] = _chunk_attention(q, 1.0 - f, i_ref[:, lanes], jnp.log(f), st_ref[h],
                                        causal, HGRN_CHUNK, group)
        o_ref[:, lanes] = (_rms(o, ng_ref[...]) * _silu(gate_ref[:, lanes])).astype(o_ref.dtype)

    def head_group(hg, carry):
        for u in range(heads_per_step):
            one_head(hg * heads_per_step + u)
        return carry

    lax.fori_loop(0, q_ref.shape[1] // (d * heads_per_step), head_group, 0)


def _hgrn_branch(z, layer, lb_param, norm_g, *, col0=0, t=256, heads_per_step=4, group=2):
    s = z.shape[0]
    depth, width = lb_param.shape
    b0 = col0 // width
    return pl.pallas_call(
        functools.partial(_hgrn_kernel, t=t, layer=layer, heads_per_step=heads_per_step, group=group),
        grid=(s // t,),
        in_specs=[pl.BlockSpec((t, width), lambda i: (i, b0)),
                  pl.BlockSpec((t, width), lambda i: (i, b0 + 1)),
                  pl.BlockSpec((t, width), lambda i: (i, b0 + 2)),
                  pl.BlockSpec((t, width), lambda i: (i, b0 + 3)),
                  pl.BlockSpec((depth, width), lambda i: (0, 0)),
                  pl.BlockSpec((None, 1, HGRN_EXPAND), lambda i: (layer, 0, 0))],
        out_specs=pl.BlockSpec((t, width), lambda i: (i, 0)),
        out_shape=jax.ShapeDtypeStruct((s, width), BF16),
        scratch_shapes=[pltpu.VMEM((width // HGRN_EXPAND, HGRN_EXPAND, HGRN_EXPAND), F32)],
        compiler_params=_params("arbitrary"),
        name="hgrn_branch",
    )(z, z, z, z, lb_param, norm_g)


def _merge_kernel(xn_ref, wm0_ref, wm1_ref, wm2_ref, y0_ref, y1_ref, y2_ref, wb_ref, o_ref, wbb_ref):
    @pl.when(pl.program_id(1) == 0)
    def _():
        wbb_ref[...] = wb_ref[...].astype(BF16)

    xn = xn_ref[...]
    acc = None
    for b, (wm_ref, y_ref) in enumerate(((wm0_ref, y0_ref), (wm1_ref, y1_ref), (wm2_ref, y2_ref))):
        term = _sigmoid(_dot(xn, wm_ref[...])) * _dot(y_ref[...], wbb_ref[b])
        acc = term if acc is None else acc + term
    o_ref[...] = acc.astype(o_ref.dtype)


def _merge(xn, w_gate, gate_col0, ys, w_branch, layer, *, tm=512, tn=256):
    s, d = xn.shape
    cw = ys[0].shape[1]
    nb = d // tn
    assert gate_col0 % tn == 0

    def wm_spec(b):
        return pl.BlockSpec((None, d, tn), lambda j, i: (layer, 0, gate_col0 // tn + b * nb + j))

    y_spec = pl.BlockSpec((tm, cw), lambda j, i: (i, 0))
    return pl.pallas_call(
        _merge_kernel,
        grid=(nb, s // tm),
        in_specs=[pl.BlockSpec((tm, d), lambda j, i: (i, 0)),
                  wm_spec(0), wm_spec(1), wm_spec(2),
                  y_spec, y_spec, y_spec,
                  pl.BlockSpec((None, N_BRANCH, cw, tn), lambda j, i: (layer, 0, 0, j))],
        out_specs=pl.BlockSpec((tm, tn), lambda j, i: (i, j)),
        out_shape=jax.ShapeDtypeStruct((s, d), BF16),
        scratch_shapes=[pltpu.VMEM((N_BRANCH, cw, tn), BF16)],
        compiler_params=_params("parallel", "arbitrary"),
        name="gated_merge",
    )(xn, w_gate, w_gate, w_gate, *ys, w_branch)


def _xattn_kernel(h_ref, g_ref, wq_ref, kv_ref, wo_ref, gn_ref, o_ref, n_ref):
    h = h_ref[...]
    q = _dot(_rms(h, g_ref[...]).astype(BF16), wq_ref[...]).astype(BF16)
    xd = wq_ref.shape[1]
    hd = xd // XATTN_HEADS
    outs = []
    for a in range(XATTN_HEADS):
        s = _dot_nt(q[:, a * hd:(a + 1) * hd], kv_ref[:, a * hd:(a + 1) * hd]) * (hd ** -0.5)
        e = jnp.exp(s - jnp.max(s, axis=-1, keepdims=True))
        p = e / jnp.sum(e, axis=-1, keepdims=True)
        outs.append(_dot(p.astype(BF16), kv_ref[:, xd + a * hd:xd + (a + 1) * hd]))
    o = jnp.concatenate(outs, axis=-1).astype(BF16)
    h_new = h + _dot(o, wo_ref[...])
    o_ref[...] = h_new
    n_ref[...] = _rms(h_new, gn_ref[...]).astype(n_ref.dtype)


def _cross_attention(h, norm_g, wq, kv, wo, layer, next_g, next_layer, next_dtype, *, tm=256):
    s, d = h.shape
    xd = wq.shape[-1]
    once = pl.Buffered(1)
    row = pl.BlockSpec((tm, d), lambda i: (i, 0))
    return pl.pallas_call(
        _xattn_kernel,
        grid=(s // tm,),
        in_specs=[row,
                  pl.BlockSpec((None, 1, d), lambda i: (layer, 0, 0)),
                  pl.BlockSpec((None, d, xd), lambda i: (layer, 0, 0), pipeline_mode=once),
                  pl.BlockSpec(kv.shape, lambda i: (0, 0), pipeline_mode=once),
                  pl.BlockSpec((None, xd, d), lambda i: (layer, 0, 0), pipeline_mode=once),
                  pl.BlockSpec((None, 1, d), lambda i: (next_layer, 0, 0))],
        out_specs=[row, row],
        out_shape=[jax.ShapeDtypeStruct((s, d), F32), jax.ShapeDtypeStruct((s, d), next_dtype)],
        compiler_params=_params("parallel"),
        name="cross_attention",
    )(h, norm_g, wq, kv, wo, next_g)


def kernel(x, mem, norm1_g, w_in, conv_w, conv_b, conv_ln_g, conv_ln_b, conv_pw, gla_wa2, gla_ba,
           gla_norm_g, hgrn_lb, hgrn_norm_g, w_branch, w_out, norm2_g, mem_norm_g, xq, xkv, xo,
           final_norm_g):
    bsz, seq, d = x.shape
    assert bsz == 1
    depth = w_in.shape[0]
    cw = conv_pw.shape[-1]
    key_w = gla_wa2.shape[-1]
    val_w = gla_norm_g.shape[-1] * GLA_HEADS
    lowrank = gla_wa2.shape[1]
    hg_w = hgrn_lb.shape[-1]
    col_gla = 3 * cw
    col_lr = col_gla + 2 * key_w + 2 * val_w
    col_hgrn = col_lr + lowrank
    assert col_lr % LANES == 0 and lowrank <= LANES

    def vec(a):
        return a.reshape(a.shape[0], 1, a.shape[-1])

    w_tail = _realign_columns(w_in, col_hgrn, 4 * hg_w + N_BRANCH * d)
    wa2 = jnp.pad(gla_wa2, ((0, 0), (0, LANES - lowrank), (0, 0))).astype(BF16)
    wq = xq.astype(BF16)
    wkv = xkv.astype(BF16)
    wxo = xo.astype(BF16)
    g1 = vec(norm1_g)
    gf = final_norm_g.reshape(1, 1, d)

    memn = _rmsnorm(mem[0], mem_norm_g.reshape(1, 1, d), 0, BF16, tm=mem.shape[1])
    h = x[0]
    xn = _rmsnorm(h, g1, 0, BF16)
    for l in range(depth):
        z = _matmul(xn, w_in, l, tm=512, tn=768, ncols=col_lr, name="in_proj")
        z_lr = _matmul(xn, w_in, l, tm=1024, tn=LANES, col0=col_lr, ncols=LANES, name="in_proj_lowrank")
        z_h = _matmul(xn, w_tail, l, tm=512, tn=1024, ncols=4 * hg_w, name="in_proj_hgrn")
        y_conv = _conv_branch(z, l, conv_w, vec(conv_b), vec(conv_ln_g), vec(conv_ln_b), conv_pw)
        y_gla = _gla_branch(z, z_lr, l, wa2, vec(gla_ba), vec(gla_norm_g),
                            col0=col_gla, key_width=key_w, value_width=val_w)
        y_hgrn = _hgrn_branch(z_h, l, hgrn_lb, vec(hgrn_norm_g))
        m = _merge(xn, w_tail, 4 * hg_w, (y_conv, y_gla, y_hgrn), w_branch, l)
        h = _matmul(m, w_out, l, tm=1024, tn=512, residual=h, name="out_proj")
        kv = _matmul(memn, wkv, l, tm=mem.shape[1], tn=1024, out_dtype=BF16, name="kv_proj")
        last = l == depth - 1
        h, xn = _cross_attention(h, vec(norm2_g), wq, kv, wxo, l,
                                 gf if last else g1, 0 if last else l + 1, F32 if last else BF16)
    return xn.reshape(bsz, seq, d)
```

```python
import functools

import jax
import jax.numpy as jnp
from jax import lax
from jax.experimental import pallas as pl
from jax.experimental.pallas import tpu as pltpu

F32 = jnp.float32
BF16 = jnp.bfloat16

EPS = 1e-6
CONV_KERNEL = 31
CONV_HALO = 32
GLA_HEADS = 4
GLA_TAU = 16.0
GLA_CHUNK = 64
HGRN_EXPAND = 128
HGRN_CHUNK = 32
N_BRANCH = 3
XATTN_HEADS = 4
LANES = 128
SUBLANES = 8
VMEM_LIMIT_BYTES = 56 * 1024 * 1024


def _params(*sem):
    return pltpu.CompilerParams(dimension_semantics=sem, vmem_limit_bytes=VMEM_LIMIT_BYTES)


def _sigmoid(x):
    return 0.5 * jnp.tanh(0.5 * x) + 0.5


def _silu(x):
    return x * _sigmoid(x)


def _dot(a, b):
    return jnp.dot(a, b, preferred_element_type=F32)


def _dot_nt(a, b):
    return lax.dot_general(a, b, (((1,), (1,)), ((), ())), preferred_element_type=F32)


def _dot_tn(a, b):
    return lax.dot_general(a, b, (((0,), (0,)), ((), ())), preferred_element_type=F32)


def _rms(x, g):
    ms = jnp.mean(x * x, axis=-1, keepdims=True)
    return (x * lax.rsqrt(ms + EPS)) * g


def _rms_kernel(x_ref, g_ref, o_ref):
    o_ref[...] = _rms(x_ref[...], g_ref[...]).astype(o_ref.dtype)


def _rmsnorm(x, g, layer, out_dtype, tm=256):
    m, d = x.shape
    return pl.pallas_call(
        _rms_kernel,
        grid=(m // tm,),
        in_specs=[pl.BlockSpec((tm, d), lambda i: (i, 0)),
                  pl.BlockSpec((None, 1, d), lambda i: (layer, 0, 0))],
        out_specs=pl.BlockSpec((tm, d), lambda i: (i, 0)),
        out_shape=jax.ShapeDtypeStruct((m, d), out_dtype),
        compiler_params=_params("parallel"),
        name="rmsnorm",
    )(x, g)


ROW_ALIGN = 16


def _shifted_rows(a_ref, b_ref, shift):
    if shift == 0:
        return a_ref[...]
    return jnp.concatenate([a_ref[...], b_ref[...]], axis=0)[shift:shift + a_ref.shape[0]]


def _row_window_specs(row0, tn, cols, index):
    shift = row0 % tn
    base = row0 - shift
    assert shift % SUBLANES == 0 and shift <= ROW_ALIGN and base % tn == 0 and tn % ROW_ALIGN == 0

    def spec(rows, row_block):
        return pl.BlockSpec((None, rows, cols), lambda *g: (index(*g)[0], row_block(index(*g)[1]), index(*g)[2]))

    specs = [spec(tn, lambda j: base // tn + j)]
    if shift:
        specs.append(spec(ROW_ALIGN, lambda j: (base + (j + 1) * tn) // ROW_ALIGN))
    return specs, shift


def _stage_kernel(*refs, shift):
    o_ref = refs[-1]
    o_ref[...] = _shifted_rows(refs[0], refs[1] if shift else None, shift).T.astype(o_ref.dtype)


def _stage_weights(wt, row0, nrows, *, tk=512, tn=1024):
    depth, _, k = wt.shape
    specs, shift = _row_window_specs(row0, tn, tk, lambda l, r, j: (l, j, r))
    return pl.pallas_call(
        functools.partial(_stage_kernel, shift=shift),
        grid=(depth, k // tk, nrows // tn),
        in_specs=specs,
        out_specs=pl.BlockSpec((None, tk, tn), lambda l, r, j: (l, r, j)),
        out_shape=jax.ShapeDtypeStruct((depth, k, nrows), BF16),
        compiler_params=_params("parallel", "parallel", "parallel"),
        name="stage_weights",
    )(*([wt] * len(specs)))


def _mm_kernel(x_ref, *rest, residual, n_weight_refs, shift, transposed):
    rest = list(rest)
    w_refs = [rest.pop(0) for _ in range(n_weight_refs)]
    r_ref = rest.pop(0) if residual else None
    o_ref = rest.pop(0)
    if rest:
        wb_ref = rest.pop(0)

        @pl.when(pl.program_id(1) == 0)
        def _():
            if transposed:
                wb_ref[...] = _shifted_rows(w_refs[0], w_refs[-1], shift).T.astype(BF16)
            else:
                wb_ref[...] = w_refs[0][...].astype(BF16)

        w = wb_ref[...]
    else:
        w = w_refs[0][...]
    acc = _dot(x_ref[...], w)
    if residual:
        acc = r_ref[...] + acc
    o_ref[...] = acc.astype(o_ref.dtype)


def _matmul(x, w, layer, *, tm, tn, col0=0, ncols=None, transposed=False, out_dtype=F32, residual=None,
            name="matmul"):
    m, k = x.shape
    n = w.shape[1] if transposed else w.shape[2]
    ncols = n - col0 if ncols is None else ncols
    assert ncols % tn == 0
    staged = w.dtype != BF16
    assert staged or not transposed
    shift = 0
    if transposed:
        w_specs, shift = _row_window_specs(col0, tn, k, lambda j, i: (layer, j, 0))
    else:
        assert col0 % tn == 0
        w_specs = [pl.BlockSpec((None, k, tn), lambda j, i: (layer, 0, col0 // tn + j))]
    in_specs = [pl.BlockSpec((tm, k), lambda j, i: (i, 0))] + w_specs
    args = [x] + [w] * len(w_specs)
    if residual is not None:
        in_specs.append(pl.BlockSpec((tm, tn), lambda j, i: (i, j)))
        args.append(residual)
    return pl.pallas_call(
        functools.partial(_mm_kernel, residual=residual is not None, n_weight_refs=len(w_specs),
                          shift=shift, transposed=transposed),
        grid=(ncols // tn, m // tm),
        in_specs=in_specs,
        out_specs=pl.BlockSpec((tm, tn), lambda j, i: (i, j)),
        out_shape=jax.ShapeDtypeStruct((m, ncols), out_dtype),
        scratch_shapes=[pltpu.VMEM((k, tn), BF16)] if staged else [],
        compiler_params=_params("parallel", "arbitrary"),
        name=name,
    )(*args)


def _conv_kernel(val_ref, glu_ref, gate_ref, pval_ref, pglu_ref, cw_ref, cb_ref, lg_ref, lb_ref,
                 pw_ref, o_ref, buf_ref, acc_ref, pwb_ref, *, tm, width):
    rows = 64
    i = pl.program_id(0)

    @pl.when(i == 0)
    def _():
        pwb_ref[...] = pw_ref[...].astype(BF16)

    a_prev = pval_ref[...] * _sigmoid(pglu_ref[...])
    buf_ref[0:CONV_HALO, :] = jnp.where(i > 0, a_prev, 0.0)
    buf_ref[CONV_HALO:CONV_HALO + tm, :] = val_ref[...] * _sigmoid(glu_ref[...])
    first = CONV_HALO - (CONV_KERNEL - 1)
    strip = rows + CONV_HALO + SUBLANES

    def lane_block(cb, carry):
        lanes = pl.ds(pl.multiple_of(cb * LANES, LANES), LANES)
        for rb in range(tm // rows):
            base = min(rb * rows, CONV_HALO + tm - strip)
            off = rb * rows - base
            x = buf_ref[base:base + strip, lanes]
            acc = jnp.zeros((rows, LANES), F32)
            for r in range(SUBLANES):
                xr = x if r == 0 else pltpu.roll(x, strip - r, axis=0)
                for j in range(strip // SUBLANES):
                    k = r + SUBLANES * j - first
                    if 0 <= k < CONV_KERNEL:
                        lo = off + SUBLANES * j
                        acc = acc + cw_ref[k:k + 1, lanes] * xr[lo:lo + rows, :]
            acc_ref[rb * rows:(rb + 1) * rows, lanes] = acc + cb_ref[:, lanes]
        return carry

    lax.fori_loop(0, width // LANES, lane_block, 0)

    c = acc_ref[...]
    mu = jnp.mean(c, axis=-1, keepdims=True)
    xc = c - mu
    var = jnp.mean(xc * xc, axis=-1, keepdims=True)
    y = (xc * lax.rsqrt(var + EPS)) * lg_ref[...] + lb_ref[...]
    pw_out = _dot(_silu(y).astype(BF16), pwb_ref[...])
    o_ref[...] = (pw_out * _silu(gate_ref[...])).astype(o_ref.dtype)


def _conv_branch(z, layer, conv_w, conv_b, ln_g, ln_b, pw, *, tm=256):
    s = z.shape[0]
    width = pw.shape[-1]
    halo_blocks = tm // CONV_HALO
    vec = pl.BlockSpec((None, 1, width), lambda i: (layer, 0, 0))
    return pl.pallas_call(
        functools.partial(_conv_kernel, tm=tm, width=width),
        grid=(s // tm,),
        in_specs=[pl.BlockSpec((tm, width), lambda i: (i, 0)),
                  pl.BlockSpec((tm, width), lambda i: (i, 1)),
                  pl.BlockSpec((tm, width), lambda i: (i, 2)),
                  pl.BlockSpec((CONV_HALO, width), lambda i: (jnp.maximum(i * halo_blocks - 1, 0), 0)),
                  pl.BlockSpec((CONV_HALO, width), lambda i: (jnp.maximum(i * halo_blocks - 1, 0), 1)),
                  pl.BlockSpec((None, CONV_KERNEL, width), lambda i: (layer, 0, 0)),
                  vec, vec, vec,
                  pl.BlockSpec((None, width, width), lambda i: (layer, 0, 0), pipeline_mode=pl.Buffered(1))],
        out_specs=pl.BlockSpec((tm, width), lambda i: (i, 0)),
        out_shape=jax.ShapeDtypeStruct((s, width), BF16),
        scratch_shapes=[pltpu.VMEM((CONV_HALO + tm, width), F32),
                        pltpu.VMEM((tm, width), F32),
                        pltpu.VMEM((width, width), BF16)],
        compiler_params=_params("arbitrary"),
        name="conv_branch",
    )(z, z, z, z, z, conv_w, conv_b, ln_g, ln_b, pw)


def _chunk_masks(t, chunk):
    row = lax.broadcasted_iota(jnp.int32, (t, t), 0)
    col = lax.broadcasted_iota(jnp.int32, (t, t), 1)
    return ((row // chunk) == (col // chunk)) & (col <= row)


def _chunk_cumsum(g, chunk):
    pos = lax.broadcasted_iota(jnp.int32, g.shape, 0) % chunk
    b = g
    s = 1
    while s < chunk:
        b = b + jnp.where(pos >= s, pltpu.roll(b, s, axis=0), 0.0)
        s *= 2
    return b


def _chunk_row(b, chunk, idx):
    t, w = b.shape
    b3 = b.reshape(t // chunk, chunk, w)
    return jnp.broadcast_to(b3[:, idx:idx + 1, :], b3.shape).reshape(t, w)


def _chunk_attention(q, k, v, g, state, causal, chunk, group):
    t, dk = q.shape
    dv = v.shape[1]
    n_chunks = t // chunk
    span = group * chunk
    b = _chunk_cumsum(g, chunk)
    d_mid = b - _chunk_row(b, chunk, chunk // 2 - 1)
    d_end = _chunk_row(b, chunk, chunk - 1) - b
    vb = v.astype(BF16)
    scores = _dot_nt((q * jnp.exp(d_mid)).astype(BF16), (k * jnp.exp(-d_mid)).astype(BF16))
    o_intra = _dot(jnp.where(causal, scores, 0.0).astype(BF16), vb)
    q_in = (q * jnp.exp(b)).astype(BF16)
    k_out = (k * jnp.exp(d_end)).astype(BF16)

    slot = lax.broadcasted_iota(jnp.int32, (span, dk), 0) // chunk
    incs = []
    for n in range(n_chunks // group):
        rows = slice(n * span, (n + 1) * span)
        ko = k_out[rows]
        if group == 1:
            incs.append(_dot_tn(vb[rows], ko))
            continue
        blocks = jnp.concatenate([jnp.where(slot == u, ko, jnp.zeros_like(ko)) for u in range(group)], axis=1)
        u_all = _dot_tn(vb[rows], blocks)
        incs.extend(u_all[:, u * dk:(u + 1) * dk] for u in range(group))

    starts = []
    for c in range(n_chunks):
        starts.append(state)
        decay = jnp.exp(b[(c + 1) * chunk - 1:(c + 1) * chunk, :])
        state = state * decay + incs[c]

    outs = []
    for n in range(n_chunks // group):
        rows = slice(n * span, (n + 1) * span)
        if group == 1:
            outs.append(o_intra[rows] + _dot_nt(q_in[rows], starts[n].astype(BF16)))
            continue
        stacked = jnp.concatenate(starts[n * group:(n + 1) * group], axis=0).astype(BF16)
        r = _dot_nt(q_in[rows], stacked)
        for u in range(group):
            sub = slice(n * span + u * chunk, n * span + (u + 1) * chunk)
            outs.append(o_intra[sub] + r[u * chunk:(u + 1) * chunk, u * dv:(u + 1) * dv])
    return jnp.concatenate(outs, axis=0), state


def _gla_kernel(q_ref, k_ref, v_ref, gate_ref, lr_ref, wa2_ref, ba_ref, ng_ref, o_ref, st_ref, *, t):
    @pl.when(pl.program_id(0) == 0)
    def _():
        st_ref[...] = jnp.zeros_like(st_ref)

    dk = q_ref.shape[1] // GLA_HEADS
    dv = v_ref.shape[1] // GLA_HEADS
    causal = _chunk_masks(t, GLA_CHUNK)
    x = _dot(lr_ref[...].astype(BF16), wa2_ref[...]) + ba_ref[...]
    g_all = (jnp.minimum(x, 0.0) - jnp.log1p(jnp.exp(-jnp.abs(x)))) * (1.0 / GLA_TAU)
    for h in range(GLA_HEADS):
        ks = slice(h * dk, (h + 1) * dk)
        vs = slice(h * dv, (h + 1) * dv)
        q = q_ref[:, ks] * (dk ** -0.5)
        o, st_ref[h] = _chunk_attention(q, k_ref[:, ks], v_ref[:, vs], g_all[:, ks], st_ref[h],
                                        causal, GLA_CHUNK, 1)
        o_ref[:, vs] = (_rms(o, ng_ref[...]) * _silu(gate_ref[:, vs])).astype(o_ref.dtype)


def _gla_branch(z, z_lr, layer, wa2, ba, norm_g, *, col0, key_width, value_width, t=256):
    s = z.shape[0]
    dk = key_width // GLA_HEADS
    dv = value_width // GLA_HEADS
    qb = col0 // key_width
    vb = (col0 + 2 * key_width) // value_width
    return pl.pallas_call(
        functools.partial(_gla_kernel, t=t),
        grid=(s // t,),
        in_specs=[pl.BlockSpec((t, key_width), lambda i: (i, qb)),
                  pl.BlockSpec((t, key_width), lambda i: (i, qb + 1)),
                  pl.BlockSpec((t, value_width), lambda i: (i, vb)),
                  pl.BlockSpec((t, value_width), lambda i: (i, vb + 1)),
                  pl.BlockSpec((t, LANES), lambda i: (i, 0)),
                  pl.BlockSpec((None, LANES, key_width), lambda i: (layer, 0, 0)),
                  pl.BlockSpec((None, 1, key_width), lambda i: (layer, 0, 0)),
                  pl.BlockSpec((None, 1, dv), lambda i: (layer, 0, 0))],
        out_specs=pl.BlockSpec((t, value_width), lambda i: (i, 0)),
        out_shape=jax.ShapeDtypeStruct((s, value_width), BF16),
        scratch_shapes=[pltpu.VMEM((GLA_HEADS, dv, dk), F32)],
        compiler_params=_params("arbitrary"),
        name="gla_branch",
    )(z, z, z, z, z_lr, wa2, ba, norm_g)


def _hgrn_kernel(q_ref, f_ref, i_ref, gate_ref, lbp_ref, ng_ref, o_ref, st_ref, *, t, layer, heads_per_step, group):
    @pl.when(pl.program_id(0) == 0)
    def _():
        st_ref[...] = jnp.zeros_like(st_ref)

    d = HGRN_EXPAND
    depth = lbp_ref.shape[0]
    causal = _chunk_masks(t, HGRN_CHUNK)

    def one_head(h):
        lanes = pl.ds(pl.multiple_of(h * d, d), d)
        p = lbp_ref[:, lanes]
        e = jnp.exp(p - jnp.max(p, axis=0, keepdims=True))
        sm = e / jnp.sum(e, axis=0, keepdims=True)
        layer_ids = lax.broadcasted_iota(jnp.int32, (depth, d), 0)
        lb = jnp.sum(jnp.where((layer_ids >= 1) & (layer_ids <= layer), sm, 0.0), axis=0, keepdims=True)
        q = _silu(q_ref[:, lanes]) * (d ** -0.5)
        f = lb + (1.0 - lb) * (1.0 / (1.0 + jnp.exp(-f_ref[:, lanes])))
        o, st_ref[h] = _chunk_attention(q, 1.0 - f, i_ref[:, lanes], jnp.log(f), st_ref[h],
                                        causal, HGRN_CHUNK, group)
        o_ref[:, lanes] = (_rms(o, ng_ref[...]) * _silu(gate_ref[:, lanes])).astype(o_ref.dtype)

    def head_group(hg, carry):
        for u in range(heads_per_step):
            one_head(hg * heads_per_step + u)
        return carry

    lax.fori_loop(0, q_ref.shape[1] // (d * heads_per_step), head_group, 0)


def _hgrn_branch(z, layer, lb_param, norm_g, *, col0=0, t=256, heads_per_step=4, group=2):
    s = z.shape[0]
    depth, width = lb_param.shape
    b0 = col0 // width
    return pl.pallas_call(
        functools.partial(_hgrn_kernel, t=t, layer=layer, heads_per_step=heads_per_step, group=group),
        grid=(s // t,),
        in_specs=[pl.BlockSpec((t, width), lambda i: (i, b0)),
                  pl.BlockSpec((t, width), lambda i: (i, b0 + 1)),
                  pl.BlockSpec((t, width), lambda i: (i, b0 + 2)),
                  pl.BlockSpec((t, width), lambda i: (i, b0 + 3)),
                  pl.BlockSpec((depth, width), lambda i: (0, 0)),
                  pl.BlockSpec((None, 1, HGRN_EXPAND), lambda i: (layer, 0, 0))],
        out_specs=pl.BlockSpec((t, width), lambda i: (i, 0)),
        out_shape=jax.ShapeDtypeStruct((s, width), BF16),
        scratch_shapes=[pltpu.VMEM((width // HGRN_EXPAND, HGRN_EXPAND, HGRN_EXPAND), F32)],
        compiler_params=_params("arbitrary"),
        name="hgrn_branch",
    )(z, z, z, z, lb_param, norm_g)


def _merge_kernel(xn_ref, wm0_ref, wm1_ref, wm2_ref, y0_ref, y1_ref, y2_ref, wb_ref, o_ref, wbb_ref):
    @pl.when(pl.program_id(1) == 0)
    def _():
        wbb_ref[...] = wb_ref[...].astype(BF16)

    xn = xn_ref[...]
    acc = None
    for b, (wm_ref, y_ref) in enumerate(((wm0_ref, y0_ref), (wm1_ref, y1_ref), (wm2_ref, y2_ref))):
        term = _sigmoid(_dot(xn, wm_ref[...])) * _dot(y_ref[...], wbb_ref[b])
        acc = term if acc is None else acc + term
    o_ref[...] = acc.astype(o_ref.dtype)


def _merge(xn, w_gate, gate_col0, ys, w_branch, layer, *, tm=512, tn=256):
    s, d = xn.shape
    cw = ys[0].shape[1]
    nb = d // tn
    assert gate_col0 % tn == 0

    def wm_spec(b):
        return pl.BlockSpec((None, d, tn), lambda j, i: (layer, 0, gate_col0 // tn + b * nb + j))

    y_spec = pl.BlockSpec((tm, cw), lambda j, i: (i, 0))
    return pl.pallas_call(
        _merge_kernel,
        grid=(nb, s // tm),
        in_specs=[pl.BlockSpec((tm, d), lambda j, i: (i, 0)),
                  wm_spec(0), wm_spec(1), wm_spec(2),
                  y_spec, y_spec, y_spec,
                  pl.BlockSpec((None, N_BRANCH, cw, tn), lambda j, i: (layer, 0, 0, j))],
        out_specs=pl.BlockSpec((tm, tn), lambda j, i: (i, j)),
        out_shape=jax.ShapeDtypeStruct((s, d), BF16),
        scratch_shapes=[pltpu.VMEM((N_BRANCH, cw, tn), BF16)],
        compiler_params=_params("parallel", "arbitrary"),
        name="gated_merge",
    )(xn, w_gate, w_gate, w_gate, *ys, w_branch)


def _xattn_kernel(h_ref, g_ref, wq_ref, kv_ref, wo_ref, gn_ref, o_ref, n_ref):
    h = h_ref[...]
    q = _dot(_rms(h, g_ref[...]).astype(BF16), wq_ref[...]).astype(BF16)
    xd = wq_ref.shape[1]
    hd = xd // XATTN_HEADS
    outs = []
    for a in range(XATTN_HEADS):
        s = _dot_nt(q[:, a * hd:(a + 1) * hd], kv_ref[:, a * hd:(a + 1) * hd]) * (hd ** -0.5)
        e = jnp.exp(s - jnp.max(s, axis=-1, keepdims=True))
        p = e / jnp.sum(e, axis=-1, keepdims=True)
        outs.append(_dot(p.astype(BF16), kv_ref[:, xd + a * hd:xd + (a + 1) * hd]))
    o = jnp.concatenate(outs, axis=-1).astype(BF16)
    h_new = h + _dot(o, wo_ref[...])
    o_ref[...] = h_new
    n_ref[...] = _rms(h_new, gn_ref[...]).astype(n_ref.dtype)


def _cross_attention(h, norm_g, wq, kv, wo, layer, next_g, next_layer, next_dtype, *, tm=256):
    s, d = h.shape
    xd = wq.shape[-1]
    once = pl.Buffered(1)
    row = pl.BlockSpec((tm, d), lambda i: (i, 0))
    return pl.pallas_call(
        _xattn_kernel,
        grid=(s // tm,),
        in_specs=[row,
                  pl.BlockSpec((None, 1, d), lambda i: (layer, 0, 0)),
                  pl.BlockSpec((None, d, xd), lambda i: (layer, 0, 0), pipeline_mode=once),
                  pl.BlockSpec(kv.shape, lambda i: (0, 0), pipeline_mode=once),
                  pl.BlockSpec((None, xd, d), lambda i: (layer, 0, 0), pipeline_mode=once),
                  pl.BlockSpec((None, 1, d), lambda i: (next_layer, 0, 0))],
        out_specs=[row, row],
        out_shape=[jax.ShapeDtypeStruct((s, d), F32), jax.ShapeDtypeStruct((s, d), next_dtype)],
        compiler_params=_params("parallel"),
        name="cross_attention",
    )(h, norm_g, wq, kv, wo, next_g)


def kernel(x, mem, norm1_g, w_in, conv_w, conv_b, conv_ln_g, conv_ln_b, conv_pw, gla_wa2, gla_ba,
           gla_norm_g, hgrn_lb, hgrn_norm_g, w_branch, w_out, norm2_g, mem_norm_g, xq, xkv, xo,
           final_norm_g):
    bsz, seq, d = x.shape
    assert bsz == 1
    depth = w_in.shape[0]
    cw = conv_pw.shape[-1]
    key_w = gla_wa2.shape[-1]
    val_w = gla_norm_g.shape[-1] * GLA_HEADS
    lowrank = gla_wa2.shape[1]
    hg_w = hgrn_lb.shape[-1]
    col_gla = 3 * cw
    col_lr = col_gla + 2 * key_w + 2 * val_w
    col_hgrn = col_lr + lowrank
    assert col_lr % LANES == 0 and lowrank <= LANES

    def vec(a):
        return a.reshape(a.shape[0], 1, a.shape[-1])

    w_in_t = jnp.swapaxes(w_in, 1, 2)
    w_gate = _stage_weights(w_in_t, col_hgrn + 4 * hg_w, N_BRANCH * d)
    wa2 = jnp.pad(gla_wa2, ((0, 0), (0, LANES - lowrank), (0, 0))).astype(BF16)
    wq = xq.astype(BF16)
    wkv = xkv.astype(BF16)
    wxo = xo.astype(BF16)
    g1 = vec(norm1_g)
    gf = final_norm_g.reshape(1, 1, d)

    memn = _rmsnorm(mem[0], mem_norm_g.reshape(1, 1, d), 0, BF16, tm=mem.shape[1])
    h = x[0]
    xn = _rmsnorm(h, g1, 0, BF16)
    for l in range(depth):
        z = _matmul(xn, w_in_t, l, tm=1024, tn=512, ncols=col_lr, transposed=True, name="in_proj")
        z_lr = _matmul(xn, w_in_t, l, tm=1024, tn=LANES, col0=col_lr, ncols=LANES, transposed=True,
                       name="in_proj_lowrank")
        z_h = _matmul(xn, w_in_t, l, tm=1024, tn=512, col0=col_hgrn, ncols=4 * hg_w, transposed=True,
                      name="in_proj_hgrn")
        y_conv = _conv_branch(z, l, conv_w, vec(conv_b), vec(conv_ln_g), vec(conv_ln_b), conv_pw)
        y_gla = _gla_branch(z, z_lr, l, wa2, vec(gla_ba), vec(gla_norm_g),
                            col0=col_gla, key_width=key_w, value_width=val_w)
        y_hgrn = _hgrn_branch(z_h, l, hgrn_lb, vec(hgrn_norm_g))
        m = _merge(xn, w_gate, 0, (y_conv, y_gla, y_hgrn), w_branch, l)
        h = _matmul(m, w_out, l, tm=1024, tn=512, residual=h, name="out_proj")
        kv = _matmul(memn, wkv, l, tm=mem.shape[1], tn=1024, out_dtype=BF16, name="kv_proj")
        last = l == depth - 1
        h, xn = _cross_attention(h, vec(norm2_g), wq, kv, wxo, l,
                                 gf if last else g1, 0 if last else l + 1, F32 if last else BF16)
    return xn.reshape(bsz, seq, d)
```

```python
import functools

import jax
import jax.numpy as jnp
from jax import lax
from jax.experimental import pallas as pl
from jax.experimental.pallas import tpu as pltpu

F32 = jnp.float32
BF16 = jnp.bfloat16

EPS = 1e-6
CONV_KERNEL = 31
CONV_HALO = 32
GLA_HEADS = 4
GLA_TAU = 16.0
GLA_CHUNK = 64
HGRN_EXPAND = 128
HGRN_CHUNK = 32
N_BRANCH = 3
XATTN_HEADS = 4
LANES = 128
SUBLANES = 8
VMEM_LIMIT_BYTES = 56 * 1024 * 1024


def _params(*sem):
    return pltpu.CompilerParams(dimension_semantics=sem, vmem_limit_bytes=VMEM_LIMIT_BYTES)


def _sigmoid(x):
    return 0.5 * jnp.tanh(0.5 * x) + 0.5


def _silu(x):
    return x * _sigmoid(x)


def _dot(a, b):
    return jnp.dot(a, b, preferred_element_type=F32)


def _dot_nt(a, b):
    return lax.dot_general(a, b, (((1,), (1,)), ((), ())), preferred_element_type=F32)


def _dot_tn(a, b):
    return lax.dot_general(a, b, (((0,), (0,)), ((), ())), preferred_element_type=F32)


def _rms(x, g):
    ms = jnp.mean(x * x, axis=-1, keepdims=True)
    return (x * lax.rsqrt(ms + EPS)) * g


def _rms_kernel(x_ref, g_ref, o_ref):
    o_ref[...] = _rms(x_ref[...], g_ref[...]).astype(o_ref.dtype)


def _rmsnorm(x, g, layer, out_dtype, tm=256):
    m, d = x.shape
    return pl.pallas_call(
        _rms_kernel,
        grid=(m // tm,),
        in_specs=[pl.BlockSpec((tm, d), lambda i: (i, 0)),
                  pl.BlockSpec((None, 1, d), lambda i: (layer, 0, 0))],
        out_specs=pl.BlockSpec((tm, d), lambda i: (i, 0)),
        out_shape=jax.ShapeDtypeStruct((m, d), out_dtype),
        compiler_params=_params("parallel"),
        name="rmsnorm",
    )(x, g)


ROW_ALIGN = 16


def _shifted_rows(a_ref, b_ref, shift):
    if shift == 0:
        return a_ref[...]
    return jnp.concatenate([a_ref[...], b_ref[...]], axis=0)[shift:shift + a_ref.shape[0]]


def _row_window_specs(row0, tn, cols, index):
    shift = row0 % tn
    base = row0 - shift
    assert shift % SUBLANES == 0 and shift <= ROW_ALIGN and base % tn == 0 and tn % ROW_ALIGN == 0

    def spec(rows, row_block):
        return pl.BlockSpec((None, rows, cols), lambda *g: (index(*g)[0], row_block(index(*g)[1]), index(*g)[2]))

    specs = [spec(tn, lambda j: base // tn + j)]
    if shift:
        specs.append(spec(ROW_ALIGN, lambda j: (base + (j + 1) * tn) // ROW_ALIGN))
    return specs, shift


def _stage_kernel(*refs, shift):
    o_ref = refs[-1]
    o_ref[...] = _shifted_rows(refs[0], refs[1] if shift else None, shift).T.astype(o_ref.dtype)


def _stage_weights(wt, row0, nrows, *, tk=2048, tn=512):
    depth, _, k = wt.shape
    specs, shift = _row_window_specs(row0, tn, tk, lambda l, r, j: (l, j, r))
    return pl.pallas_call(
        functools.partial(_stage_kernel, shift=shift),
        grid=(depth, k // tk, nrows // tn),
        in_specs=specs,
        out_specs=pl.BlockSpec((None, tk, tn), lambda l, r, j: (l, r, j)),
        out_shape=jax.ShapeDtypeStruct((depth, k, nrows), BF16),
        compiler_params=_params("parallel", "parallel", "parallel"),
        name="stage_weights",
    )(*([wt] * len(specs)))


def _mm_kernel(x_ref, *rest, residual, n_weight_refs, shift, transposed):
    rest = list(rest)
    w_refs = [rest.pop(0) for _ in range(n_weight_refs)]
    r_ref = rest.pop(0) if residual else None
    o_ref = rest.pop(0)
    if rest:
        wb_ref = rest.pop(0)

        @pl.when(pl.program_id(1) == 0)
        def _():
            if transposed:
                wb_ref[...] = _shifted_rows(w_refs[0], w_refs[-1], shift).T.astype(BF16)
            else:
                wb_ref[...] = w_refs[0][...].astype(BF16)

        w = wb_ref[...]
    else:
        w = w_refs[0][...]
    acc = _dot(x_ref[...], w)
    if residual:
        acc = r_ref[...] + acc
    o_ref[...] = acc.astype(o_ref.dtype)


def _matmul(x, w, layer, *, tm, tn, col0=0, ncols=None, transposed=False, out_dtype=F32, residual=None,
            name="matmul"):
    m, k = x.shape
    n = w.shape[1] if transposed else w.shape[2]
    ncols = n - col0 if ncols is None else ncols
    assert ncols % tn == 0
    staged = w.dtype != BF16
    assert staged or not transposed
    shift = 0
    if transposed:
        w_specs, shift = _row_window_specs(col0, tn, k, lambda j, i: (layer, j, 0))
    else:
        assert col0 % tn == 0
        w_specs = [pl.BlockSpec((None, k, tn), lambda j, i: (layer, 0, col0 // tn + j))]
    in_specs = [pl.BlockSpec((tm, k), lambda j, i: (i, 0))] + w_specs
    args = [x] + [w] * len(w_specs)
    if residual is not None:
        in_specs.append(pl.BlockSpec((tm, tn), lambda j, i: (i, j)))
        args.append(residual)
    return pl.pallas_call(
        functools.partial(_mm_kernel, residual=residual is not None, n_weight_refs=len(w_specs),
                          shift=shift, transposed=transposed),
        grid=(ncols // tn, m // tm),
        in_specs=in_specs,
        out_specs=pl.BlockSpec((tm, tn), lambda j, i: (i, j)),
        out_shape=jax.ShapeDtypeStruct((m, ncols), out_dtype),
        scratch_shapes=[pltpu.VMEM((k, tn), BF16)] if staged else [],
        compiler_params=_params("parallel", "arbitrary"),
        name=name,
    )(*args)


def _conv_kernel(val_ref, glu_ref, gate_ref, pval_ref, pglu_ref, cw_ref, cb_ref, lg_ref, lb_ref,
                 pw_ref, o_ref, buf_ref, acc_ref, pwb_ref, *, tm, width):
    rows = 64
    i = pl.program_id(0)

    @pl.when(i == 0)
    def _():
        pwb_ref[...] = pw_ref[...].astype(BF16)

    a_prev = pval_ref[...] * _sigmoid(pglu_ref[...])
    buf_ref[0:CONV_HALO, :] = jnp.where(i > 0, a_prev, 0.0)
    buf_ref[CONV_HALO:CONV_HALO + tm, :] = val_ref[...] * _sigmoid(glu_ref[...])
    first = CONV_HALO - (CONV_KERNEL - 1)
    strip = rows + CONV_HALO + SUBLANES

    def lane_block(cb, carry):
        lanes = pl.ds(pl.multiple_of(cb * LANES, LANES), LANES)
        for rb in range(tm // rows):
            base = min(rb * rows, CONV_HALO + tm - strip)
            off = rb * rows - base
            x = buf_ref[base:base + strip, lanes]
            acc = jnp.zeros((rows, LANES), F32)
            for r in range(SUBLANES):
                xr = x if r == 0 else pltpu.roll(x, strip - r, axis=0)
                for j in range(strip // SUBLANES):
                    k = r + SUBLANES * j - first
                    if 0 <= k < CONV_KERNEL:
                        lo = off + SUBLANES * j
                        acc = acc + cw_ref[k:k + 1, lanes] * xr[lo:lo + rows, :]
            acc_ref[rb * rows:(rb + 1) * rows, lanes] = acc + cb_ref[:, lanes]
        return carry

    lax.fori_loop(0, width // LANES, lane_block, 0)

    c = acc_ref[...]
    mu = jnp.mean(c, axis=-1, keepdims=True)
    xc = c - mu
    var = jnp.mean(xc * xc, axis=-1, keepdims=True)
    y = (xc * lax.rsqrt(var + EPS)) * lg_ref[...] + lb_ref[...]
    pw_out = _dot(_silu(y).astype(BF16), pwb_ref[...])
    o_ref[...] = (pw_out * _silu(gate_ref[...])).astype(o_ref.dtype)


def _conv_branch(z, layer, conv_w, conv_b, ln_g, ln_b, pw, *, tm=256):
    s = z.shape[0]
    width = pw.shape[-1]
    halo_blocks = tm // CONV_HALO
    vec = pl.BlockSpec((None, 1, width), lambda i: (layer, 0, 0))
    return pl.pallas_call(
        functools.partial(_conv_kernel, tm=tm, width=width),
        grid=(s // tm,),
        in_specs=[pl.BlockSpec((tm, width), lambda i: (i, 0)),
                  pl.BlockSpec((tm, width), lambda i: (i, 1)),
                  pl.BlockSpec((tm, width), lambda i: (i, 2)),
                  pl.BlockSpec((CONV_HALO, width), lambda i: (jnp.maximum(i * halo_blocks - 1, 0), 0)),
                  pl.BlockSpec((CONV_HALO, width), lambda i: (jnp.maximum(i * halo_blocks - 1, 0), 1)),
                  pl.BlockSpec((None, CONV_KERNEL, width), lambda i: (layer, 0, 0)),
                  vec, vec, vec,
                  pl.BlockSpec((None, width, width), lambda i: (layer, 0, 0), pipeline_mode=pl.Buffered(1))],
        out_specs=pl.BlockSpec((tm, width), lambda i: (i, 0)),
        out_shape=jax.ShapeDtypeStruct((s, width), BF16),
        scratch_shapes=[pltpu.VMEM((CONV_HALO + tm, width), F32),
                        pltpu.VMEM((tm, width), F32),
                        pltpu.VMEM((width, width), BF16)],
        compiler_params=_params("arbitrary"),
        name="conv_branch",
    )(z, z, z, z, z, conv_w, conv_b, ln_g, ln_b, pw)


def _chunk_masks(t, chunk):
    row = lax.broadcasted_iota(jnp.int32, (t, t), 0)
    col = lax.broadcasted_iota(jnp.int32, (t, t), 1)
    return ((row // chunk) == (col // chunk)) & (col <= row)


def _chunk_cumsum(g, chunk):
    pos = lax.broadcasted_iota(jnp.int32, g.shape, 0) % chunk
    b = g
    s = 1
    while s < chunk:
        b = b + jnp.where(pos >= s, pltpu.roll(b, s, axis=0), 0.0)
        s *= 2
    return b


def _chunk_row(b, chunk, idx):
    t, w = b.shape
    b3 = b.reshape(t // chunk, chunk, w)
    return jnp.broadcast_to(b3[:, idx:idx + 1, :], b3.shape).reshape(t, w)


def _chunk_attention(q, k, v, g, state, causal, chunk, group):
    outs, states = _chunk_attention_multi([q], [k], [v], [g], [state], causal, chunk, group)
    return outs[0], states[0]


def _chunk_attention_multi(qs, ks, vs, gs, states, causal, chunk, group, fillers=()):
    fillers = list(fillers)

    def fill():
        if fillers:
            fillers.pop(0)()

    t, dk = qs[0].shape
    dv = vs[0].shape[1]
    n_chunks = t // chunk
    span = group * chunk
    nh = len(qs)
    fill()
    bs, vbs, q_ins, k_outs, scores = [], [], [], [], []
    for h in range(nh):
        b = _chunk_cumsum(gs[h], chunk)
        d_mid = b - _chunk_row(b, chunk, chunk // 2 - 1)
        d_end = _chunk_row(b, chunk, chunk - 1) - b
        bs.append(b)
        vbs.append(vs[h].astype(BF16))
        scores.append(_dot_nt((qs[h] * jnp.exp(d_mid)).astype(BF16), (ks[h] * jnp.exp(-d_mid)).astype(BF16)))
        q_ins.append((qs[h] * jnp.exp(b)).astype(BF16))
        k_outs.append((ks[h] * jnp.exp(d_end)).astype(BF16))
    fill()
    o_intras = [_dot(jnp.where(causal, scores[h], 0.0).astype(BF16), vbs[h]) for h in range(nh)]
    fill()

    slot = lax.broadcasted_iota(jnp.int32, (span, dk), 0) // chunk
    incs = [[] for _ in range(nh)]
    for h in range(nh):
        for n in range(n_chunks // group):
            rows = slice(n * span, (n + 1) * span)
            ko = k_outs[h][rows]
            if group == 1:
                incs[h].append(_dot_tn(vbs[h][rows], ko))
                continue
            blocks = jnp.concatenate([jnp.where(slot == u, ko, jnp.zeros_like(ko)) for u in range(group)], axis=1)
            u_all = _dot_tn(vbs[h][rows], blocks)
            incs[h].extend(u_all[:, u * dk:(u + 1) * dk] for u in range(group))
    fill()

    results, new_states = [], []
    for h in range(nh):
        state = states[h]
        starts = []
        for c in range(n_chunks):
            starts.append(state)
            decay = jnp.exp(bs[h][(c + 1) * chunk - 1:(c + 1) * chunk, :])
            state = state * decay + incs[h][c]
        outs = []
        for n in range(n_chunks // group):
            rows = slice(n * span, (n + 1) * span)
            if group == 1:
                outs.append(o_intras[h][rows] + _dot_nt(q_ins[h][rows], starts[n].astype(BF16)))
                continue
            stacked = jnp.concatenate(starts[n * group:(n + 1) * group], axis=0).astype(BF16)
            r = _dot_nt(q_ins[h][rows], stacked)
            for u in range(group):
                sub = slice(n * span + u * chunk, n * span + (u + 1) * chunk)
                outs.append(o_intras[h][sub] + r[u * chunk:(u + 1) * chunk, u * dv:(u + 1) * dv])
        results.append(jnp.concatenate(outs, axis=0))
        new_states.append(state)
    return results, new_states


def _gla_kernel(q_ref, k_ref, v_ref, gate_ref, lr_ref, wa2_ref, ba_ref, ng_ref, o_ref, st_ref, *, t):
    @pl.when(pl.program_id(0) == 0)
    def _():
        st_ref[...] = jnp.zeros_like(st_ref)

    dk = q_ref.shape[1] // GLA_HEADS
    dv = v_ref.shape[1] // GLA_HEADS
    causal = _chunk_masks(t, GLA_CHUNK)
    x = _dot(lr_ref[...].astype(BF16), wa2_ref[...]) + ba_ref[...]
    g_all = (jnp.minimum(x, 0.0) - jnp.log1p(jnp.exp(-jnp.abs(x)))) * (1.0 / GLA_TAU)
    for h in range(GLA_HEADS):
        ks = slice(h * dk, (h + 1) * dk)
        vs = slice(h * dv, (h + 1) * dv)
        q = q_ref[:, ks] * (dk ** -0.5)
        o, st_ref[h] = _chunk_attention(q, k_ref[:, ks], v_ref[:, vs], g_all[:, ks], st_ref[h],
                                        causal, GLA_CHUNK, 1)
        o_ref[:, vs] = (_rms(o, ng_ref[...]) * _silu(gate_ref[:, vs])).astype(o_ref.dtype)


def _gla_branch(z, z_lr, layer, wa2, ba, norm_g, *, col0, key_width, value_width, t=256):
    s = z.shape[0]
    dk = key_width // GLA_HEADS
    dv = value_width // GLA_HEADS
    qb = col0 // key_width
    vb = (col0 + 2 * key_width) // value_width
    return pl.pallas_call(
        functools.partial(_gla_kernel, t=t),
        grid=(s // t,),
        in_specs=[pl.BlockSpec((t, key_width), lambda i: (i, qb)),
                  pl.BlockSpec((t, key_width), lambda i: (i, qb + 1)),
                  pl.BlockSpec((t, value_width), lambda i: (i, vb)),
                  pl.BlockSpec((t, value_width), lambda i: (i, vb + 1)),
                  pl.BlockSpec((t, LANES), lambda i: (i, 0)),
                  pl.BlockSpec((None, LANES, key_width), lambda i: (layer, 0, 0)),
                  pl.BlockSpec((None, 1, key_width), lambda i: (layer, 0, 0)),
                  pl.BlockSpec((None, 1, dv), lambda i: (layer, 0, 0))],
        out_specs=pl.BlockSpec((t, value_width), lambda i: (i, 0)),
        out_shape=jax.ShapeDtypeStruct((s, value_width), BF16),
        scratch_shapes=[pltpu.VMEM((GLA_HEADS, dv, dk), F32)],
        compiler_params=_params("arbitrary"),
        name="gla_branch",
    )(z, z, z, z, z_lr, wa2, ba, norm_g)


def _hgrn_fused_kernel(x_ref, *refs, t, layer, heads, group, n_weight_refs, shift):
    w_refs = refs[:n_weight_refs]
    lbp_ref, ng_ref, o_ref, wb_ref, za_ref, zb_ref, st_ref = refs[n_weight_refs:]
    i = pl.program_id(1)
    d = HGRN_EXPAND
    gw = heads * d
    per = n_weight_refs // 4

    @pl.when(i == 0)
    def _():
        for a in range(4):
            pair = w_refs[a * per:(a + 1) * per]
            wb_ref[:, a * gw:(a + 1) * gw] = _shifted_rows(pair[0], pair[-1], shift).T.astype(BF16)
        zb_ref[...] = jnp.zeros_like(zb_ref)
        st_ref[...] = jnp.zeros_like(st_ref)

    depth = lbp_ref.shape[0]

    def step(z_new, z_old):
        def project(a):
            def job():
                z_new[:, a * gw:(a + 1) * gw] = _dot(x_ref[...], wb_ref[:, a * gw:(a + 1) * gw])
            return job

        causal = _chunk_masks(t, HGRN_CHUNK)
        qs, ks, gs = [], [], []
        for u in range(heads):
            p = lbp_ref[:, u * d:(u + 1) * d]
            e = jnp.exp(p - jnp.max(p, axis=0, keepdims=True))
            sm = e / jnp.sum(e, axis=0, keepdims=True)
            layer_ids = lax.broadcasted_iota(jnp.int32, (depth, d), 0)
            lb = jnp.sum(jnp.where((layer_ids >= 1) & (layer_ids <= layer), sm, 0.0), axis=0, keepdims=True)
            qs.append(_silu(z_old[:, u * d:(u + 1) * d]) * (d ** -0.5))
            f = lb + (1.0 - lb) * (1.0 / (1.0 + jnp.exp(-z_old[:, gw + u * d:gw + (u + 1) * d])))
            ks.append(1.0 - f)
            gs.append(jnp.log(f))
        vs = [z_old[:, 2 * gw + u * d:2 * gw + (u + 1) * d] for u in range(heads)]
        outs, states = _chunk_attention_multi(qs, ks, vs, gs, [st_ref[u] for u in range(heads)], causal,
                                              HGRN_CHUNK, group, fillers=[project(a) for a in range(4)])
        for u in range(heads):
            st_ref[u] = states[u]
            gate = z_old[:, 3 * gw + u * d:3 * gw + (u + 1) * d]
            o_ref[:, u * d:(u + 1) * d] = (_rms(outs[u], ng_ref[...]) * _silu(gate)).astype(o_ref.dtype)

    @pl.when(i % 2 == 0)
    def _():
        step(za_ref, zb_ref)

    @pl.when(i % 2 == 1)
    def _():
        step(zb_ref, za_ref)


def _hgrn_fused(xn, wt, layer, lb_param, norm_g, *, row0, t=256, heads=2, group=2):
    s, k = xn.shape
    depth, width = lb_param.shape
    gw = heads * HGRN_EXPAND
    n = s // t
    w_specs = []
    for a in range(4):
        specs, shift = _row_window_specs(row0 + a * width, gw, k, lambda g, i: (layer, g, 0))
        w_specs += specs
    return pl.pallas_call(
        functools.partial(_hgrn_fused_kernel, t=t, layer=layer, heads=heads, group=group,
                          n_weight_refs=len(w_specs), shift=shift),
        grid=(width // gw, n + 1),
        in_specs=[pl.BlockSpec((t, k), lambda g, i: (jnp.minimum(i, n - 1), 0))] + w_specs
                 + [pl.BlockSpec((depth, gw), lambda g, i: (0, g)),
                    pl.BlockSpec((None, 1, HGRN_EXPAND), lambda g, i: (layer, 0, 0))],
        out_specs=pl.BlockSpec((t, gw), lambda g, i: (jnp.maximum(i - 1, 0), g)),
        out_shape=jax.ShapeDtypeStruct((s, width), BF16),
        scratch_shapes=[pltpu.VMEM((k, 4 * gw), BF16),
                        pltpu.VMEM((t, 4 * gw), F32),
                        pltpu.VMEM((t, 4 * gw), F32),
                        pltpu.VMEM((heads, HGRN_EXPAND, HGRN_EXPAND), F32)],
        compiler_params=_params("arbitrary", "arbitrary"),
        name="hgrn_fused",
    )(xn, *([wt] * len(w_specs)), lb_param, norm_g)


def _merge_kernel(xn_ref, wm0_ref, wm1_ref, wm2_ref, y0_ref, y1_ref, y2_ref, wb_ref, o_ref, wbb_ref):
    @pl.when(pl.program_id(1) == 0)
    def _():
        wbb_ref[...] = wb_ref[...].astype(BF16)

    xn = xn_ref[...]
    acc = None
    for b, (wm_ref, y_ref) in enumerate(((wm0_ref, y0_ref), (wm1_ref, y1_ref), (wm2_ref, y2_ref))):
        term = _sigmoid(_dot(xn, wm_ref[...])) * _dot(y_ref[...], wbb_ref[b])
        acc = term if acc is None else acc + term
    o_ref[...] = acc.astype(o_ref.dtype)


def _merge(xn, w_gate, gate_col0, ys, w_branch, layer, *, tm=512, tn=256):
    s, d = xn.shape
    cw = ys[0].shape[1]
    nb = d // tn
    assert gate_col0 % tn == 0

    def wm_spec(b):
        return pl.BlockSpec((None, d, tn), lambda j, i: (layer, 0, gate_col0 // tn + b * nb + j))

    y_spec = pl.BlockSpec((tm, cw), lambda j, i: (i, 0))
    return pl.pallas_call(
        _merge_kernel,
        grid=(nb, s // tm),
        in_specs=[pl.BlockSpec((tm, d), lambda j, i: (i, 0)),
                  wm_spec(0), wm_spec(1), wm_spec(2),
                  y_spec, y_spec, y_spec,
                  pl.BlockSpec((None, N_BRANCH, cw, tn), lambda j, i: (layer, 0, 0, j))],
        out_specs=pl.BlockSpec((tm, tn), lambda j, i: (i, j)),
        out_shape=jax.ShapeDtypeStruct((s, d), BF16),
        scratch_shapes=[pltpu.VMEM((N_BRANCH, cw, tn), BF16)],
        compiler_params=_params("parallel", "arbitrary"),
        name="gated_merge",
    )(xn, w_gate, w_gate, w_gate, *ys, w_branch)


def _xattn_kernel(h_ref, g_ref, wq_ref, kv_ref, wo_ref, gn_ref, o_ref, n_ref):
    h = h_ref[...]
    q = _dot(_rms(h, g_ref[...]).astype(BF16), wq_ref[...]).astype(BF16)
    xd = wq_ref.shape[1]
    hd = xd // XATTN_HEADS
    outs = []
    for a in range(XATTN_HEADS):
        s = _dot_nt(q[:, a * hd:(a + 1) * hd], kv_ref[:, a * hd:(a + 1) * hd]) * (hd ** -0.5)
        e = jnp.exp(s - jnp.max(s, axis=-1, keepdims=True))
        p = e / jnp.sum(e, axis=-1, keepdims=True)
        outs.append(_dot(p.astype(BF16), kv_ref[:, xd + a * hd:xd + (a + 1) * hd]))
    o = jnp.concatenate(outs, axis=-1).astype(BF16)
    h_new = h + _dot(o, wo_ref[...])
    o_ref[...] = h_new
    n_ref[...] = _rms(h_new, gn_ref[...]).astype(n_ref.dtype)


def _cross_attention(h, norm_g, wq, kv, wo, layer, next_g, next_layer, next_dtype, *, tm=256):
    s, d = h.shape
    xd = wq.shape[-1]
    once = pl.Buffered(1)
    row = pl.BlockSpec((tm, d), lambda i: (i, 0))
    return pl.pallas_call(
        _xattn_kernel,
        grid=(s // tm,),
        in_specs=[row,
                  pl.BlockSpec((None, 1, d), lambda i: (layer, 0, 0)),
                  pl.BlockSpec((None, d, xd), lambda i: (layer, 0, 0), pipeline_mode=once),
                  pl.BlockSpec(kv.shape, lambda i: (0, 0), pipeline_mode=once),
                  pl.BlockSpec((None, xd, d), lambda i: (layer, 0, 0), pipeline_mode=once),
                  pl.BlockSpec((None, 1, d), lambda i: (next_layer, 0, 0))],
        out_specs=[row, row],
        out_shape=[jax.ShapeDtypeStruct((s, d), F32), jax.ShapeDtypeStruct((s, d), next_dtype)],
        compiler_params=_params("parallel"),
        name="cross_attention",
    )(h, norm_g, wq, kv, wo, next_g)


def kernel(x, mem, norm1_g, w_in, conv_w, conv_b, conv_ln_g, conv_ln_b, conv_pw, gla_wa2, gla_ba,
           gla_norm_g, hgrn_lb, hgrn_norm_g, w_branch, w_out, norm2_g, mem_norm_g, xq, xkv, xo,
           final_norm_g):
    bsz, seq, d = x.shape
    assert bsz == 1
    depth = w_in.shape[0]
    cw = conv_pw.shape[-1]
    key_w = gla_wa2.shape[-1]
    val_w = gla_norm_g.shape[-1] * GLA_HEADS
    lowrank = gla_wa2.shape[1]
    hg_w = hgrn_lb.shape[-1]
    col_gla = 3 * cw
    col_lr = col_gla + 2 * key_w + 2 * val_w
    col_hgrn = col_lr + lowrank
    assert col_lr % LANES == 0 and lowrank <= LANES

    def vec(a):
        return a.reshape(a.shape[0], 1, a.shape[-1])

    w_in_t = jnp.swapaxes(w_in, 1, 2)
    w_gate = _stage_weights(w_in_t, col_hgrn + 4 * hg_w, N_BRANCH * d)
    wa2 = jnp.pad(gla_wa2, ((0, 0), (0, LANES - lowrank), (0, 0))).astype(BF16)
    wq = xq.astype(BF16)
    wkv = xkv.astype(BF16)
    wxo = xo.astype(BF16)
    g1 = vec(norm1_g)
    gf = final_norm_g.reshape(1, 1, d)

    memn = _rmsnorm(mem[0], mem_norm_g.reshape(1, 1, d), 0, BF16, tm=mem.shape[1])
    h = x[0]
    xn = _rmsnorm(h, g1, 0, BF16)
    for l in range(depth):
        z = _matmul(xn, w_in_t, l, tm=1024, tn=512, ncols=col_lr, transposed=True, name="in_proj")
        z_lr = _matmul(xn, w_in_t, l, tm=1024, tn=LANES, col0=col_lr, ncols=LANES, transposed=True,
                       name="in_proj_lowrank")
        y_conv = _conv_branch(z, l, conv_w, vec(conv_b), vec(conv_ln_g), vec(conv_ln_b), conv_pw)
        y_gla = _gla_branch(z, z_lr, l, wa2, vec(gla_ba), vec(gla_norm_g),
                            col0=col_gla, key_width=key_w, value_width=val_w)
        y_hgrn = _hgrn_fused(xn, w_in_t, l, hgrn_lb, vec(hgrn_norm_g), row0=col_hgrn)
        m = _merge(xn, w_gate, 0, (y_conv, y_gla, y_hgrn), w_branch, l)
        h = _matmul(m, w_out, l, tm=1024, tn=512, residual=h, name="out_proj")
        kv = _matmul(memn, wkv, l, tm=mem.shape[1], tn=1024, out_dtype=BF16, name="kv_proj")
        last = l == depth - 1
        h, xn = _cross_attention(h, vec(norm2_g), wq, kv, wxo, l,
                                 gf if last else g1, 0 if last else l + 1, F32 if last else BF16)
    return xn.reshape(bsz, seq, d)
```

```python
import functools

import jax
import jax.numpy as jnp
from jax import lax
from jax.experimental import pallas as pl
from jax.experimental.pallas import tpu as pltpu

F32 = jnp.float32
BF16 = jnp.bfloat16

EPS = 1e-6
CONV_KERNEL = 31
CONV_HALO = 32
GLA_HEADS = 4
GLA_TAU = 16.0
GLA_CHUNK = 64
HGRN_EXPAND = 128
HGRN_CHUNK = 32
N_BRANCH = 3
XATTN_HEADS = 4
LANES = 128
SUBLANES = 8
VMEM_LIMIT_BYTES = 56 * 1024 * 1024


def _params(*sem):
    return pltpu.CompilerParams(dimension_semantics=sem, vmem_limit_bytes=VMEM_LIMIT_BYTES)


def _sigmoid(x):
    return 0.5 * jnp.tanh(0.5 * x) + 0.5


def _silu(x):
    return x * _sigmoid(x)


def _dot(a, b):
    return jnp.dot(a, b, preferred_element_type=F32)


def _dot_nt(a, b):
    return lax.dot_general(a, b, (((1,), (1,)), ((), ())), preferred_element_type=F32)


def _dot_tn(a, b):
    return lax.dot_general(a, b, (((0,), (0,)), ((), ())), preferred_element_type=F32)


def _rms(x, g):
    ms = jnp.mean(x * x, axis=-1, keepdims=True)
    return (x * lax.rsqrt(ms + EPS)) * g


def _rms_kernel(x_ref, g_ref, o_ref):
    o_ref[...] = _rms(x_ref[...], g_ref[...]).astype(o_ref.dtype)


def _rmsnorm(x, g, layer, out_dtype, tm=256):
    m, d = x.shape
    return pl.pallas_call(
        _rms_kernel,
        grid=(m // tm,),
        in_specs=[pl.BlockSpec((tm, d), lambda i: (i, 0)),
                  pl.BlockSpec((None, 1, d), lambda i: (layer, 0, 0))],
        out_specs=pl.BlockSpec((tm, d), lambda i: (i, 0)),
        out_shape=jax.ShapeDtypeStruct((m, d), out_dtype),
        compiler_params=_params("parallel"),
        name="rmsnorm",
    )(x, g)


ROW_ALIGN = 16


def _shifted_rows(a_ref, b_ref, shift):
    if shift == 0:
        return a_ref[...]
    return jnp.concatenate([a_ref[...], b_ref[...]], axis=0)[shift:shift + a_ref.shape[0]]


def _row_window_specs(row0, tn, cols, index):
    shift = row0 % tn
    base = row0 - shift
    assert shift % SUBLANES == 0 and shift <= ROW_ALIGN and base % tn == 0 and tn % ROW_ALIGN == 0

    def spec(rows, row_block):
        return pl.BlockSpec((None, rows, cols), lambda *g: (index(*g)[0], row_block(index(*g)[1]), index(*g)[2]))

    specs = [spec(tn, lambda j: base // tn + j)]
    if shift:
        specs.append(spec(ROW_ALIGN, lambda j: (base + (j + 1) * tn) // ROW_ALIGN))
    return specs, shift


def _stage_kernel(*refs, shift):
    o_ref = refs[-1]
    o_ref[...] = _shifted_rows(refs[0], refs[1] if shift else None, shift).T.astype(o_ref.dtype)


def _stage_weights(wt, row0, nrows, *, tk=2048, tn=512):
    depth, _, k = wt.shape
    specs, shift = _row_window_specs(row0, tn, tk, lambda l, r, j: (l, j, r))
    return pl.pallas_call(
        functools.partial(_stage_kernel, shift=shift),
        grid=(depth, k // tk, nrows // tn),
        in_specs=specs,
        out_specs=pl.BlockSpec((None, tk, tn), lambda l, r, j: (l, r, j)),
        out_shape=jax.ShapeDtypeStruct((depth, k, nrows), BF16),
        compiler_params=_params("parallel", "parallel", "parallel"),
        name="stage_weights",
    )(*([wt] * len(specs)))


def _mm_kernel(x_ref, *rest, residual, n_weight_refs, shift, transposed):
    rest = list(rest)
    w_refs = [rest.pop(0) for _ in range(n_weight_refs)]
    r_ref = rest.pop(0) if residual else None
    o_ref = rest.pop(0)
    if rest:
        wb_ref = rest.pop(0)

        @pl.when(pl.program_id(1) == 0)
        def _():
            if transposed:
                wb_ref[...] = _shifted_rows(w_refs[0], w_refs[-1], shift).T.astype(BF16)
            else:
                wb_ref[...] = w_refs[0][...].astype(BF16)

        w = wb_ref[...]
    else:
        w = w_refs[0][...]
    acc = _dot(x_ref[...], w)
    if residual:
        acc = r_ref[...] + acc
    o_ref[...] = acc.astype(o_ref.dtype)


def _matmul(x, w, layer, *, tm, tn, col0=0, ncols=None, transposed=False, out_dtype=F32, residual=None,
            name="matmul"):
    m, k = x.shape
    n = w.shape[1] if transposed else w.shape[2]
    ncols = n - col0 if ncols is None else ncols
    assert ncols % tn == 0
    staged = w.dtype != BF16
    assert staged or not transposed
    shift = 0
    if transposed:
        w_specs, shift = _row_window_specs(col0, tn, k, lambda j, i: (layer, j, 0))
    else:
        assert col0 % tn == 0
        w_specs = [pl.BlockSpec((None, k, tn), lambda j, i: (layer, 0, col0 // tn + j))]
    in_specs = [pl.BlockSpec((tm, k), lambda j, i: (i, 0))] + w_specs
    args = [x] + [w] * len(w_specs)
    if residual is not None:
        in_specs.append(pl.BlockSpec((tm, tn), lambda j, i: (i, j)))
        args.append(residual)
    return pl.pallas_call(
        functools.partial(_mm_kernel, residual=residual is not None, n_weight_refs=len(w_specs),
                          shift=shift, transposed=transposed),
        grid=(ncols // tn, m // tm),
        in_specs=in_specs,
        out_specs=pl.BlockSpec((tm, tn), lambda j, i: (i, j)),
        out_shape=jax.ShapeDtypeStruct((m, ncols), out_dtype),
        scratch_shapes=[pltpu.VMEM((k, tn), BF16)] if staged else [],
        compiler_params=_params("parallel", "arbitrary"),
        name=name,
    )(*args)


def _conv_kernel(val_ref, glu_ref, gate_ref, pval_ref, pglu_ref, cw_ref, cb_ref, lg_ref, lb_ref,
                 pw_ref, o_ref, buf_ref, acc_ref, pwb_ref, *, tm, width):
    rows = 64
    i = pl.program_id(0)

    @pl.when(i == 0)
    def _():
        pwb_ref[...] = pw_ref[...].astype(BF16)

    a_prev = pval_ref[...] * _sigmoid(pglu_ref[...])
    buf_ref[0:CONV_HALO, :] = jnp.where(i > 0, a_prev, 0.0)
    buf_ref[CONV_HALO:CONV_HALO + tm, :] = val_ref[...] * _sigmoid(glu_ref[...])
    first = CONV_HALO - (CONV_KERNEL - 1)
    strip = rows + CONV_HALO + SUBLANES

    def lane_block(cb, carry):
        lanes = pl.ds(pl.multiple_of(cb * LANES, LANES), LANES)
        for rb in range(tm // rows):
            base = min(rb * rows, CONV_HALO + tm - strip)
            off = rb * rows - base
            x = buf_ref[base:base + strip, lanes]
            acc = jnp.zeros((rows, LANES), F32)
            for r in range(SUBLANES):
                xr = x if r == 0 else pltpu.roll(x, strip - r, axis=0)
                for j in range(strip // SUBLANES):
                    k = r + SUBLANES * j - first
                    if 0 <= k < CONV_KERNEL:
                        lo = off + SUBLANES * j
                        acc = acc + cw_ref[k:k + 1, lanes] * xr[lo:lo + rows, :]
            acc_ref[rb * rows:(rb + 1) * rows, lanes] = acc + cb_ref[:, lanes]
        return carry

    lax.fori_loop(0, width // LANES, lane_block, 0)

    c = acc_ref[...]
    mu = jnp.mean(c, axis=-1, keepdims=True)
    xc = c - mu
    var = jnp.mean(xc * xc, axis=-1, keepdims=True)
    y = (xc * lax.rsqrt(var + EPS)) * lg_ref[...] + lb_ref[...]
    pw_out = _dot(_silu(y).astype(BF16), pwb_ref[...])
    o_ref[...] = (pw_out * _silu(gate_ref[...])).astype(o_ref.dtype)


def _conv_branch(z, layer, conv_w, conv_b, ln_g, ln_b, pw, *, tm=256):
    s = z.shape[0]
    width = pw.shape[-1]
    halo_blocks = tm // CONV_HALO
    vec = pl.BlockSpec((None, 1, width), lambda i: (layer, 0, 0))
    return pl.pallas_call(
        functools.partial(_conv_kernel, tm=tm, width=width),
        grid=(s // tm,),
        in_specs=[pl.BlockSpec((tm, width), lambda i: (i, 0)),
                  pl.BlockSpec((tm, width), lambda i: (i, 1)),
                  pl.BlockSpec((tm, width), lambda i: (i, 2)),
                  pl.BlockSpec((CONV_HALO, width), lambda i: (jnp.maximum(i * halo_blocks - 1, 0), 0)),
                  pl.BlockSpec((CONV_HALO, width), lambda i: (jnp.maximum(i * halo_blocks - 1, 0), 1)),
                  pl.BlockSpec((None, CONV_KERNEL, width), lambda i: (layer, 0, 0)),
                  vec, vec, vec,
                  pl.BlockSpec((None, width, width), lambda i: (layer, 0, 0), pipeline_mode=pl.Buffered(1))],
        out_specs=pl.BlockSpec((tm, width), lambda i: (i, 0)),
        out_shape=jax.ShapeDtypeStruct((s, width), BF16),
        scratch_shapes=[pltpu.VMEM((CONV_HALO + tm, width), F32),
                        pltpu.VMEM((tm, width), F32),
                        pltpu.VMEM((width, width), BF16)],
        compiler_params=_params("arbitrary"),
        name="conv_branch",
    )(z, z, z, z, z, conv_w, conv_b, ln_g, ln_b, pw)


def _chunk_masks(t, chunk):
    row = lax.broadcasted_iota(jnp.int32, (t, t), 0)
    col = lax.broadcasted_iota(jnp.int32, (t, t), 1)
    return ((row // chunk) == (col // chunk)) & (col <= row)


def _chunk_cumsum(g, chunk):
    pos = lax.broadcasted_iota(jnp.int32, g.shape, 0) % chunk
    b = g
    s = 1
    while s < chunk:
        b = b + jnp.where(pos >= s, pltpu.roll(b, s, axis=0), 0.0)
        s *= 2
    return b


def _chunk_row(b, chunk, idx):
    t, w = b.shape
    b3 = b.reshape(t // chunk, chunk, w)
    return jnp.broadcast_to(b3[:, idx:idx + 1, :], b3.shape).reshape(t, w)


def _chunk_attention(q, k, v, g, state, causal, chunk, group):
    outs, states = _chunk_attention_multi([q], [k], [v], [g], [state], causal, chunk, group)
    return outs[0], states[0]


def _chunk_attention_multi(qs, ks, vs, gs, states, causal, chunk, group, fillers=()):
    fillers = list(fillers)

    def fill():
        if fillers:
            fillers.pop(0)()

    t, dk = qs[0].shape
    dv = vs[0].shape[1]
    n_chunks = t // chunk
    span = group * chunk
    nh = len(qs)
    fill()
    bs, vbs, q_ins, k_outs, scores = [], [], [], [], []
    for h in range(nh):
        b = _chunk_cumsum(gs[h], chunk)
        d_mid = b - _chunk_row(b, chunk, chunk // 2 - 1)
        d_end = _chunk_row(b, chunk, chunk - 1) - b
        bs.append(b)
        vbs.append(vs[h].astype(BF16))
        scores.append(_dot_nt((qs[h] * jnp.exp(d_mid)).astype(BF16), (ks[h] * jnp.exp(-d_mid)).astype(BF16)))
        q_ins.append((qs[h] * jnp.exp(b)).astype(BF16))
        k_outs.append((ks[h] * jnp.exp(d_end)).astype(BF16))
    fill()
    o_intras = [_dot(jnp.where(causal, scores[h], 0.0).astype(BF16), vbs[h]) for h in range(nh)]
    fill()

    slot = lax.broadcasted_iota(jnp.int32, (span, dk), 0) // chunk
    incs = [[] for _ in range(nh)]
    for h in range(nh):
        for n in range(n_chunks // group):
            rows = slice(n * span, (n + 1) * span)
            ko = k_outs[h][rows]
            if group == 1:
                incs[h].append(_dot_tn(vbs[h][rows], ko))
                continue
            blocks = jnp.concatenate([jnp.where(slot == u, ko, jnp.zeros_like(ko)) for u in range(group)], axis=1)
            u_all = _dot_tn(vbs[h][rows], blocks)
            incs[h].extend(u_all[:, u * dk:(u + 1) * dk] for u in range(group))
    fill()

    results, new_states = [], []
    for h in range(nh):
        state = states[h]
        starts = []
        for c in range(n_chunks):
            starts.append(state)
            decay = jnp.exp(bs[h][(c + 1) * chunk - 1:(c + 1) * chunk, :])
            state = state * decay + incs[h][c]
        outs = []
        for n in range(n_chunks // group):
            rows = slice(n * span, (n + 1) * span)
            if group == 1:
                outs.append(o_intras[h][rows] + _dot_nt(q_ins[h][rows], starts[n].astype(BF16)))
                continue
            stacked = jnp.concatenate(starts[n * group:(n + 1) * group], axis=0).astype(BF16)
            r = _dot_nt(q_ins[h][rows], stacked)
            for u in range(group):
                sub = slice(n * span + u * chunk, n * span + (u + 1) * chunk)
                outs.append(o_intras[h][sub] + r[u * chunk:(u + 1) * chunk, u * dv:(u + 1) * dv])
        results.append(jnp.concatenate(outs, axis=0))
        new_states.append(state)
    return results, new_states


def _gla_fused_kernel(x_ref, lr_ref, wq_ref, wk_ref, wv_ref, wg_ref, wa2_ref, ba_ref, ng_ref, o_ref,
                      wb_ref, za_ref, zb_ref, st_ref, *, t, dk, dv):
    i = pl.program_id(1)
    width = 2 * dk + 2 * dv

    @pl.when(i == 0)
    def _():
        wb_ref[:, 0:dk] = wq_ref[...].T.astype(BF16)
        wb_ref[:, dk:2 * dk] = wk_ref[...].T.astype(BF16)
        wb_ref[:, 2 * dk:2 * dk + dv] = wv_ref[...].T.astype(BF16)
        wb_ref[:, 2 * dk + dv:width] = wg_ref[...].T.astype(BF16)
        zb_ref[...] = jnp.zeros_like(zb_ref)
        st_ref[...] = jnp.zeros_like(st_ref)

    def step(z_new, z_old):
        bounds = (0, 2 * dk, 2 * dk + dv, 2 * dk + dv + dv // 2, width)

        def project(a):
            def job():
                lo, hi = bounds[a], bounds[a + 1]
                z_new[:, lo:hi] = _dot(x_ref[...], wb_ref[:, lo:hi])
            return job

        causal = _chunk_masks(t, GLA_CHUNK)
        x = _dot(lr_ref[...].astype(BF16), wa2_ref[...]) + ba_ref[...]
        g = (jnp.minimum(x, 0.0) - jnp.log1p(jnp.exp(-jnp.abs(x)))) * (1.0 / GLA_TAU)
        outs, states = _chunk_attention_multi(
            [z_old[:, 0:dk] * (dk ** -0.5)], [z_old[:, dk:2 * dk]], [z_old[:, 2 * dk:2 * dk + dv]], [g],
            [st_ref[...]], causal, GLA_CHUNK, 1, fillers=[project(a) for a in range(4)])
        st_ref[...] = states[0]
        o_ref[...] = (_rms(outs[0], ng_ref[...]) * _silu(z_old[:, 2 * dk + dv:width])).astype(o_ref.dtype)

    @pl.when(i % 2 == 0)
    def _():
        step(za_ref, zb_ref)

    @pl.when(i % 2 == 1)
    def _():
        step(zb_ref, za_ref)


def _gla_fused(xn, z_lr, wt, layer, wa2, ba, norm_g, *, row0, key_width, value_width, t=256):
    s, k = xn.shape
    dk = key_width // GLA_HEADS
    dv = value_width // GLA_HEADS
    n = s // t
    once = pl.Buffered(1)

    def w_spec(rows, r0):
        assert r0 % rows == 0
        return pl.BlockSpec((None, rows, k), lambda h, i: (layer, r0 // rows + h, 0), pipeline_mode=once)

    return pl.pallas_call(
        functools.partial(_gla_fused_kernel, t=t, dk=dk, dv=dv),
        grid=(GLA_HEADS, n + 1),
        in_specs=[pl.BlockSpec((t, k), lambda h, i: (jnp.minimum(i, n - 1), 0)),
                  pl.BlockSpec((t, LANES), lambda h, i: (jnp.maximum(i - 1, 0), 0)),
                  w_spec(dk, row0), w_spec(dk, row0 + key_width),
                  w_spec(dv, row0 + 2 * key_width), w_spec(dv, row0 + 2 * key_width + value_width),
                  pl.BlockSpec((None, LANES, dk), lambda h, i: (layer, 0, h)),
                  pl.BlockSpec((None, 1, dk), lambda h, i: (layer, 0, h)),
                  pl.BlockSpec((None, 1, dv), lambda h, i: (layer, 0, 0))],
        out_specs=pl.BlockSpec((t, dv), lambda h, i: (jnp.maximum(i - 1, 0), h)),
        out_shape=jax.ShapeDtypeStruct((s, value_width), BF16),
        scratch_shapes=[pltpu.VMEM((k, 2 * dk + 2 * dv), BF16),
                        pltpu.VMEM((t, 2 * dk + 2 * dv), F32),
                        pltpu.VMEM((t, 2 * dk + 2 * dv), F32),
                        pltpu.VMEM((dv, dk), F32)],
        compiler_params=_params("arbitrary", "arbitrary"),
        name="gla_fused",
    )(xn, z_lr, wt, wt, wt, wt, wa2, ba, norm_g)


def _hgrn_fused_kernel(x_ref, *refs, t, layer, heads, group, n_weight_refs, shift):
    w_refs = refs[:n_weight_refs]
    lbp_ref, ng_ref, o_ref, wb_ref, za_ref, zb_ref, st_ref = refs[n_weight_refs:]
    i = pl.program_id(1)
    d = HGRN_EXPAND
    gw = heads * d
    per = n_weight_refs // 4

    @pl.when(i == 0)
    def _():
        for a in range(4):
            pair = w_refs[a * per:(a + 1) * per]
            wb_ref[:, a * gw:(a + 1) * gw] = _shifted_rows(pair[0], pair[-1], shift).T.astype(BF16)
        zb_ref[...] = jnp.zeros_like(zb_ref)
        st_ref[...] = jnp.zeros_like(st_ref)

    depth = lbp_ref.shape[0]

    def step(z_new, z_old):
        def project(a):
            def job():
                z_new[:, a * gw:(a + 1) * gw] = _dot(x_ref[...], wb_ref[:, a * gw:(a + 1) * gw])
            return job

        causal = _chunk_masks(t, HGRN_CHUNK)
        qs, ks, gs = [], [], []
        for u in range(heads):
            p = lbp_ref[:, u * d:(u + 1) * d]
            e = jnp.exp(p - jnp.max(p, axis=0, keepdims=True))
            sm = e / jnp.sum(e, axis=0, keepdims=True)
            layer_ids = lax.broadcasted_iota(jnp.int32, (depth, d), 0)
            lb = jnp.sum(jnp.where((layer_ids >= 1) & (layer_ids <= layer), sm, 0.0), axis=0, keepdims=True)
            qs.append(_silu(z_old[:, u * d:(u + 1) * d]) * (d ** -0.5))
            f = lb + (1.0 - lb) * (1.0 / (1.0 + jnp.exp(-z_old[:, gw + u * d:gw + (u + 1) * d])))
            ks.append(1.0 - f)
            gs.append(jnp.log(f))
        vs = [z_old[:, 2 * gw + u * d:2 * gw + (u + 1) * d] for u in range(heads)]
        outs, states = _chunk_attention_multi(qs, ks, vs, gs, [st_ref[u] for u in range(heads)], causal,
                                              HGRN_CHUNK, group, fillers=[project(a) for a in range(4)])
        for u in range(heads):
            st_ref[u] = states[u]
            gate = z_old[:, 3 * gw + u * d:3 * gw + (u + 1) * d]
            o_ref[:, u * d:(u + 1) * d] = (_rms(outs[u], ng_ref[...]) * _silu(gate)).astype(o_ref.dtype)

    @pl.when(i % 2 == 0)
    def _():
        step(za_ref, zb_ref)

    @pl.when(i % 2 == 1)
    def _():
        step(zb_ref, za_ref)


def _hgrn_fused(xn, wt, layer, lb_param, norm_g, *, row0, t=256, heads=2, group=2):
    s, k = xn.shape
    depth, width = lb_param.shape
    gw = heads * HGRN_EXPAND
    n = s // t
    w_specs = []
    for a in range(4):
        specs, shift = _row_window_specs(row0 + a * width, gw, k, lambda g, i: (layer, g, 0))
        w_specs += specs
    return pl.pallas_call(
        functools.partial(_hgrn_fused_kernel, t=t, layer=layer, heads=heads, group=group,
                          n_weight_refs=len(w_specs), shift=shift),
        grid=(width // gw, n + 1),
        in_specs=[pl.BlockSpec((t, k), lambda g, i: (jnp.minimum(i, n - 1), 0))] + w_specs
                 + [pl.BlockSpec((depth, gw), lambda g, i: (0, g)),
                    pl.BlockSpec((None, 1, HGRN_EXPAND), lambda g, i: (layer, 0, 0))],
        out_specs=pl.BlockSpec((t, gw), lambda g, i: (jnp.maximum(i - 1, 0), g)),
        out_shape=jax.ShapeDtypeStruct((s, width), BF16),
        scratch_shapes=[pltpu.VMEM((k, 4 * gw), BF16),
                        pltpu.VMEM((t, 4 * gw), F32),
                        pltpu.VMEM((t, 4 * gw), F32),
                        pltpu.VMEM((heads, HGRN_EXPAND, HGRN_EXPAND), F32)],
        compiler_params=_params("arbitrary", "arbitrary"),
        name="hgrn_fused",
    )(xn, *([wt] * len(w_specs)), lb_param, norm_g)


def _merge_kernel(xn_ref, wm0_ref, wm1_ref, wm2_ref, y0_ref, y1_ref, y2_ref, wb_ref, o_ref, wbb_ref):
    @pl.when(pl.program_id(1) == 0)
    def _():
        wbb_ref[...] = wb_ref[...].astype(BF16)

    xn = xn_ref[...]
    acc = None
    for b, (wm_ref, y_ref) in enumerate(((wm0_ref, y0_ref), (wm1_ref, y1_ref), (wm2_ref, y2_ref))):
        term = _sigmoid(_dot(xn, wm_ref[...])) * _dot(y_ref[...], wbb_ref[b])
        acc = term if acc is None else acc + term
    o_ref[...] = acc.astype(o_ref.dtype)


def _merge(xn, w_gate, gate_col0, ys, w_branch, layer, *, tm=512, tn=256):
    s, d = xn.shape
    cw = ys[0].shape[1]
    nb = d // tn
    assert gate_col0 % tn == 0

    def wm_spec(b):
        return pl.BlockSpec((None, d, tn), lambda j, i: (layer, 0, gate_col0 // tn + b * nb + j))

    y_spec = pl.BlockSpec((tm, cw), lambda j, i: (i, 0))
    return pl.pallas_call(
        _merge_kernel,
        grid=(nb, s // tm),
        in_specs=[pl.BlockSpec((tm, d), lambda j, i: (i, 0)),
                  wm_spec(0), wm_spec(1), wm_spec(2),
                  y_spec, y_spec, y_spec,
                  pl.BlockSpec((None, N_BRANCH, cw, tn), lambda j, i: (layer, 0, 0, j))],
        out_specs=pl.BlockSpec((tm, tn), lambda j, i: (i, j)),
        out_shape=jax.ShapeDtypeStruct((s, d), BF16),
        scratch_shapes=[pltpu.VMEM((N_BRANCH, cw, tn), BF16)],
        compiler_params=_params("parallel", "arbitrary"),
        name="gated_merge",
    )(xn, w_gate, w_gate, w_gate, *ys, w_branch)


def _xattn_kernel(h_ref, g_ref, wq_ref, kv_ref, wo_ref, gn_ref, o_ref, n_ref):
    h = h_ref[...]
    q = _dot(_rms(h, g_ref[...]).astype(BF16), wq_ref[...]).astype(BF16)
    xd = wq_ref.shape[1]
    hd = xd // XATTN_HEADS
    outs = []
    for a in range(XATTN_HEADS):
        s = _dot_nt(q[:, a * hd:(a + 1) * hd], kv_ref[:, a * hd:(a + 1) * hd]) * (hd ** -0.5)
        e = jnp.exp(s - jnp.max(s, axis=-1, keepdims=True))
        p = e / jnp.sum(e, axis=-1, keepdims=True)
        outs.append(_dot(p.astype(BF16), kv_ref[:, xd + a * hd:xd + (a + 1) * hd]))
    o = jnp.concatenate(outs, axis=-1).astype(BF16)
    h_new = h + _dot(o, wo_ref[...])
    o_ref[...] = h_new
    n_ref[...] = _rms(h_new, gn_ref[...]).astype(n_ref.dtype)


def _cross_attention(h, norm_g, wq, kv, wo, layer, next_g, next_layer, next_dtype, *, tm=256):
    s, d = h.shape
    xd = wq.shape[-1]
    once = pl.Buffered(1)
    row = pl.BlockSpec((tm, d), lambda i: (i, 0))
    return pl.pallas_call(
        _xattn_kernel,
        grid=(s // tm,),
        in_specs=[row,
                  pl.BlockSpec((None, 1, d), lambda i: (layer, 0, 0)),
                  pl.BlockSpec((None, d, xd), lambda i: (layer, 0, 0), pipeline_mode=once),
                  pl.BlockSpec(kv.shape, lambda i: (0, 0), pipeline_mode=once),
                  pl.BlockSpec((None, xd, d), lambda i: (layer, 0, 0), pipeline_mode=once),
                  pl.BlockSpec((None, 1, d), lambda i: (next_layer, 0, 0))],
        out_specs=[row, row],
        out_shape=[jax.ShapeDtypeStruct((s, d), F32), jax.ShapeDtypeStruct((s, d), next_dtype)],
        compiler_params=_params("parallel"),
        name="cross_attention",
    )(h, norm_g, wq, kv, wo, next_g)


def kernel(x, mem, norm1_g, w_in, conv_w, conv_b, conv_ln_g, conv_ln_b, conv_pw, gla_wa2, gla_ba,
           gla_norm_g, hgrn_lb, hgrn_norm_g, w_branch, w_out, norm2_g, mem_norm_g, xq, xkv, xo,
           final_norm_g):
    bsz, seq, d = x.shape
    assert bsz == 1
    depth = w_in.shape[0]
    cw = conv_pw.shape[-1]
    key_w = gla_wa2.shape[-1]
    val_w = gla_norm_g.shape[-1] * GLA_HEADS
    lowrank = gla_wa2.shape[1]
    hg_w = hgrn_lb.shape[-1]
    col_gla = 3 * cw
    col_lr = col_gla + 2 * key_w + 2 * val_w
    col_hgrn = col_lr + lowrank
    assert col_lr % LANES == 0 and lowrank <= LANES

    def vec(a):
        return a.reshape(a.shape[0], 1, a.shape[-1])

    w_in_t = jnp.swapaxes(w_in, 1, 2)
    w_gate = _stage_weights(w_in_t, col_hgrn + 4 * hg_w, N_BRANCH * d)
    wa2 = jnp.pad(gla_wa2, ((0, 0), (0, LANES - lowrank), (0, 0))).astype(BF16)
    wq = xq.astype(BF16)
    wkv = xkv.astype(BF16)
    wxo = xo.astype(BF16)
    g1 = vec(norm1_g)
    gf = final_norm_g.reshape(1, 1, d)

    memn = _rmsnorm(mem[0], mem_norm_g.reshape(1, 1, d), 0, BF16, tm=mem.shape[1])
    h = x[0]
    xn = _rmsnorm(h, g1, 0, BF16)
    for l in range(depth):
        z = _matmul(xn, w_in_t, l, tm=1024, tn=512, ncols=col_gla, transposed=True, name="in_proj")
        z_lr = _matmul(xn, w_in_t, l, tm=1024, tn=LANES, col0=col_lr, ncols=LANES, transposed=True,
                       name="in_proj_lowrank")
        y_conv = _conv_branch(z, l, conv_w, vec(conv_b), vec(conv_ln_g), vec(conv_ln_b), conv_pw)
        y_gla = _gla_fused(xn, z_lr, w_in_t, l, wa2, vec(gla_ba), vec(gla_norm_g),
                           row0=col_gla, key_width=key_w, value_width=val_w)
        y_hgrn = _hgrn_fused(xn, w_in_t, l, hgrn_lb, vec(hgrn_norm_g), row0=col_hgrn)
        m = _merge(xn, w_gate, 0, (y_conv, y_gla, y_hgrn), w_branch, l)
        h = _matmul(m, w_out, l, tm=1024, tn=512, residual=h, name="out_proj")
        kv = _matmul(memn, wkv, l, tm=mem.shape[1], tn=1024, out_dtype=BF16, name="kv_proj")
        last = l == depth - 1
        h, xn = _cross_attention(h, vec(norm2_g), wq, kv, wxo, l,
                                 gf if last else g1, 0 if last else l + 1, F32 if last else BF16)
    return xn.reshape(bsz, seq, d)
```
